```python
import math
import jax, jax.numpy as jnp
from jax import lax
import numpy as np

D_MODEL = 1024
BATCH = 2
SEQ = 8192
DEPTH = 2
DEC_BATCH = 32
DEC_SEQ = 8
PAST_LEN = 16384
PAGE_SIZE = 128

G_HEADS = 4
G_DK = 128
G_DV = 128
G_WK = G_HEADS * G_DK
G_WV = G_HEADS * G_DV
GDN_CHUNK = 64
CONV_W = 4
GDN_CONV_CH = 2 * G_WK + G_WV
A_HEADS = 4
A_DQK = 64
A_DV = 2 * A_DQK
A_WQK = A_HEADS * 2 * A_DQK
A_WV = A_HEADS * A_DV
Q_BLOCK = 128
NUM_BUCKETS = 32
MAX_DISTANCE = 128
L_WIDTH = 512
L_BLOCKS = 8
L_BLOCK_DIM = L_WIDTH // L_BLOCKS
LRU_C = 8.0
D_FF = 2816
N_EXPERTS = 8
TOP_K = 2
D_FF_EXPERT = 3584
EPS = 1e-6
IN_SIZES = (G_WK, G_WK, G_WV, G_WV, G_HEADS, G_HEADS, A_WQK, A_WQK, A_WV, L_WIDTH, L_WIDTH)
D_IN = G_WK * 2 + G_WV * 2 + G_HEADS * 2 + A_WQK * 2 + A_WV + L_WIDTH * 2

kernel_name = 'hybrid_gdn_diffattn_rglru_step'

F32 = jnp.float32


def rmsnorm(x, g):
    xf = x.astype(F32)
    y = xf * lax.rsqrt(jnp.mean(xf * xf, axis=-1, keepdims=True) + EPS)
    return (y * g.astype(F32)).astype(x.dtype)


def l2norm(x):
    return x * lax.rsqrt(jnp.sum(x * x, axis=-1, keepdims=True) + 1e-6)


def split_cols(u):
    idx, acc = [], 0
    for s in IN_SIZES[:-1]:
        acc += s
        idx.append(acc)
    return jnp.split(u, idx, axis=-1)


def causal_conv(x, buf, w):
    ext = jnp.concatenate([buf.astype(x.dtype), x], axis=1)
    T = x.shape[1]
    y = ext[:, 0:T] * w[0]
    for j in range(1, CONV_W):
        y = y + ext[:, j:j + T] * w[j]
    return y, ext[:, -(CONV_W - 1):]


def gdn_chunked(q, k, v, g, beta, s0):
    B, T, H, DK = k.shape
    DV = v.shape[-1]
    C = GDN_CHUNK
    pad = (-T) % C
    N = (T + pad) // C

    def prep(a):
        a = jnp.pad(a, [(0, 0), (0, pad)] + [(0, 0)] * (a.ndim - 2))
        a = jnp.moveaxis(a, 2, 1)
        return a.reshape(a.shape[:2] + (N, C) + a.shape[3:])

    qc, kc, vc, gc, bc = [prep(a) for a in (q, k, v, g, beta)]
    gc = jnp.cumsum(gc, axis=-1)
    tril = jnp.tril(jnp.ones((C, C), bool))
    strict = jnp.tril(jnp.ones((C, C), bool), -1)
    gdiff = gc[..., :, None] - gc[..., None, :]
    decay = jnp.exp(jnp.where(tril, gdiff, -jnp.inf))
    kb = kc * bc[..., None]
    lmat = jnp.where(strict, jnp.einsum('bhncd,bhnsd->bhncs', kb, kc) * decay, 0.0)
    eye = jnp.eye(C, dtype=F32)
    tinv = lax.linalg.triangular_solve(eye + lmat, jnp.broadcast_to(eye, lmat.shape),
                                       left_side=True, lower=True, unit_diagonal=True)
    u = tinv @ (vc * bc[..., None])
    w = tinv @ (kb * jnp.exp(gc)[..., None])
    a_intra = jnp.einsum('bhncd,bhnsd->bhncs', qc, kc) * decay
    q_dec = qc * jnp.exp(gc)[..., None]
    k_dec = kc * jnp.exp(gc[..., -1:] - gc)[..., None]
    g_last = jnp.exp(gc[..., -1])

    def step(S, xs):
        qd, kd, u_i, w_i, a_i, gl = xs
        v_new = u_i - w_i @ S
        o = qd @ S + a_i @ v_new
        S = S * gl[..., None, None] + jnp.einsum('bhcd,bhce->bhde', kd, v_new)
        return S, o

    xs = tuple(jnp.moveaxis(t, 2, 0) for t in (q_dec, k_dec, u, w, a_intra, g_last))
    S, o = lax.scan(step, s0, xs)
    o = jnp.moveaxis(o, 0, 2).reshape(B, H, N * C, DV)[:, :, :T]
    return jnp.moveaxis(o, 1, 2), S


def gdn_mixer(gq, gk, gv, gz, gb, ga, conv_buf, s0, conv_w, a_log, dt_bias, norm_g):
    B, T, _ = gq.shape
    qkv, new_buf = causal_conv(jnp.concatenate([gq, gk, gv], axis=-1), conv_buf, conv_w)
    qkv = jax.nn.silu(qkv.astype(F32))
    q = l2norm(qkv[..., :G_WK].reshape(B, T, G_HEADS, G_DK)) * (G_DK ** -0.5)
    k = l2norm(qkv[..., G_WK:2 * G_WK].reshape(B, T, G_HEADS, G_DK))
    v = qkv[..., 2 * G_WK:].reshape(B, T, G_HEADS, G_DV)
    beta = jax.nn.sigmoid(gb.astype(F32))
    g = -jnp.exp(a_log.astype(F32)) * jax.nn.softplus(ga.astype(F32) + dt_bias.astype(F32))
    o, s = gdn_chunked(q, k, v, g, beta, s0.astype(F32))
    z = gz.astype(F32).reshape(B, T, G_HEADS, G_DV)
    o = rmsnorm(o, norm_g) * jax.nn.silu(z)
    return o.reshape(B, T, G_WV).astype(gq.dtype), s, new_buf


def rel_bucket(dist):
    n = jnp.maximum(dist, 0)
    max_exact = NUM_BUCKETS // 2
    nf = jnp.maximum(n, 1).astype(F32)
    large = max_exact + (jnp.log(nf / max_exact) / math.log(MAX_DISTANCE / max_exact)
                         * (NUM_BUCKETS - max_exact)).astype(jnp.int32)
    large = jnp.minimum(large, NUM_BUCKETS - 1)
    return jnp.where(n < max_exact, n, large)


def diff_attn_block(q, k, v, q_pos, k_pos, lam, rel_bias):
    s = jnp.einsum('bqhcd,bkhcd->bhcqk', q, k).astype(F32) * (A_DQK ** -0.5)
    dist = q_pos[:, None] - k_pos[None, :]
    bias = jnp.take(rel_bias.astype(F32).T, rel_bucket(dist), axis=1)
    s = s + bias[None, :, None]
    s = jnp.where((dist >= 0)[None, None, None], s, -jnp.inf)
    p = jax.nn.softmax(s, axis=-1)
    a = p[:, :, 0] - lam * p[:, :, 1]
    return jnp.einsum('bhqk,bkhd->bqhd', a.astype(v.dtype), v)


def diff_attention(q, k, v, q_pos, k_pos, lam, rel_bias):
    B, T = q.shape[:2]
    if T > Q_BLOCK and T % Q_BLOCK == 0:
        nb = T // Q_BLOCK
        qb = q.reshape((B, nb, Q_BLOCK) + q.shape[2:]).swapaxes(0, 1)
        pb = q_pos.reshape(nb, Q_BLOCK)
        out = lax.map(lambda a: diff_attn_block(a[0], k, v, a[1], k_pos, lam, rel_bias), (qb, pb))
        return out.swapaxes(0, 1).reshape(B, T, A_HEADS, A_DV)
    return diff_attn_block(q, k, v, q_pos, k_pos, lam, rel_bias)


def lambda_init(layer):
    return 0.8 - 0.6 * math.exp(-0.3 * layer)


def diff_lambda(lp, layer):
    lp = lp.astype(F32)
    return jnp.exp(jnp.sum(lp[0] * lp[1])) - jnp.exp(jnp.sum(lp[2] * lp[3])) + lambda_init(layer)


def _lin_combine(left, right):
    a_l, b_l = left
    a_r, b_r = right
    return a_l * a_r, a_r * b_l + b_r


def rglru_mixer(xb, yb, conv_buf, h0, conv_w, conv_b, wa, ba, wx, bx, lam):
    xc, new_buf = causal_conv(xb, conv_buf, conv_w)
    xc = (xc + conv_b).astype(F32)
    B, T, _ = xc.shape
    xh = xc.reshape(B, T, L_BLOCKS, L_BLOCK_DIM)
    r = jax.nn.sigmoid(jnp.einsum('btnd,nde->btne', xh, wa.astype(F32)).reshape(B, T, L_WIDTH) + ba.astype(F32))
    i = jax.nn.sigmoid(jnp.einsum('btnd,nde->btne', xh, wx.astype(F32)).reshape(B, T, L_WIDTH) + bx.astype(F32))
    log_a = -LRU_C * r * jax.nn.softplus(-lam.astype(F32))
    a = jnp.exp(log_a)
    b = jnp.sqrt(-jnp.expm1(2.0 * log_a)) * (i * xc)
    b = b.at[:, 0].add(a[:, 0] * h0.astype(F32))
    _, h = lax.associative_scan(_lin_combine, (a, b), axis=1)
    out = h.astype(yb.dtype) * jax.nn.gelu(yb)
    return out, h[:, -1], new_buf


def swiglu(h, w1, w3, w2):
    return (jax.nn.silu(h @ w1) * (h @ w3)) @ w2


def moe_ffn(h, router, w1, w3, w2):
    logits = (h @ router).astype(F32)
    top_v, top_i = lax.top_k(logits, TOP_K)
    wts = jax.nn.softmax(top_v, axis=-1)
    gate = jnp.sum(jax.nn.one_hot(top_i, N_EXPERTS, dtype=F32) * wts[..., None], axis=-2).astype(h.dtype)
    out = jnp.zeros_like(h)
    for e in range(N_EXPERTS):
        out = out + gate[..., e:e + 1] * swiglu(h, w1[e], w3[e], w2[e])
    return out


def trunk(x, q_pos, cache_k, cache_v, page_table, s_gdn, s_gconv, s_lru, s_lconv, p):
    B, T, _ = x.shape
    k_rows, v_rows, gdn_s, gdn_c, lru_h, lru_c = [], [], [], [], [], []
    for l in range(DEPTH):
        h = rmsnorm(x, p['norm1_g'][l])
        gq, gk, gv, gz, gb, ga, aq, ak, av, lx, ly = split_cols(h @ p['w_in'][l])
        o_g, s_new, gc_new = gdn_mixer(gq, gk, gv, gz, gb, ga, s_gconv[l], s_gdn[l], p['g_conv_w'][l],
                                       p['g_a_log'][l], p['g_dt_bias'][l], p['g_norm_g'][l])
        qd = aq.reshape(B, T, A_HEADS, 2, A_DQK)
        kd = ak.reshape(B, T, A_HEADS, 2, A_DQK)
        vd = av.reshape(B, T, A_HEADS, A_DV)
        if cache_k is None:
            k_all, v_all, k_pos = kd, vd, q_pos
        else:
            kp = cache_k[l][page_table].reshape(B, -1, A_HEADS, 2, A_DQK).astype(kd.dtype)
            vp = cache_v[l][page_table].reshape(B, -1, A_HEADS, A_DV).astype(vd.dtype)
            k_all = jnp.concatenate([kp, kd], axis=1)
            v_all = jnp.concatenate([vp, vd], axis=1)
            k_pos = jnp.arange(k_all.shape[1], dtype=jnp.int32)
        lam = diff_lambda(p['a_lambda'][l], l)
        o_a = diff_attention(qd, k_all, v_all, q_pos, k_pos, lam, p['rel_bias'])
        o_a = (rmsnorm(o_a, p['a_subln_g'][l]) * (1.0 - lambda_init(l))).reshape(B, T, A_WV)
        o_l, h_new, lc_new = rglru_mixer(lx, ly, s_lconv[l], s_lru[l], p['l_conv_w'][l], p['l_conv_b'][l],
                                         p['l_wa'][l], p['l_ba'][l], p['l_wx'][l], p['l_bx'][l], p['l_lambda'][l])
        gates = jax.nn.sigmoid((h @ p['w_gate'][l] + p['b_gate'][l]).astype(F32)).astype(x.dtype)
        g_g, g_a, g_l = jnp.split(gates, 3, axis=-1)
        m = (g_g * (o_g @ p['w_br_g'][l]) + g_a * (o_a @ p['w_br_a'][l])
             + g_l * (o_l.astype(x.dtype) @ p['w_br_l'][l]))
        x = x + m @ p['w_out'][l]
        h2 = rmsnorm(x, p['norm2_g'][l])
        if l % 2 == 0:
            j = l // 2
            x = x + swiglu(h2, p['f_w1'][j], p['f_w3'][j], p['f_w2'][j])
        else:
            j = l // 2
            x = x + moe_ffn(h2, p['m_router'][j], p['m_w1'][j], p['m_w3'][j], p['m_w2'][j])
        k_rows.append(kd.reshape(B, T, A_HEADS, 2 * A_DQK))
        v_rows.append(vd)
        gdn_s.append(s_new.astype(x.dtype))
        gdn_c.append(gc_new)
        lru_h.append(h_new.astype(x.dtype))
        lru_c.append(lc_new)
    y = rmsnorm(x, p['final_g'])
    return (y, jnp.stack(k_rows), jnp.stack(v_rows), jnp.stack(gdn_s), jnp.stack(gdn_c),
            jnp.stack(lru_h), jnp.stack(lru_c))


def setup_inputs(seed: int = 0) -> dict:
    key = jax.random.key(seed)
    keys = iter(jax.random.split(key, 64))

    def nrm(shape, scale):
        return jax.random.normal(next(keys), shape, F32) * scale

    def unif(shape, lo, hi):
        return jax.random.uniform(next(keys), shape, F32, lo, hi)

    n_pages = PAST_LEN // PAGE_SIZE
    n_used = DEC_BATCH * n_pages
    n_pool = n_used + n_used // 4
    n_dense = (DEPTH + 1) // 2
    n_moe = DEPTH // 2
    page_table = jax.random.permutation(next(keys), n_pool)[:n_used].reshape(DEC_BATCH, n_pages).astype(jnp.int32)
    dt = jnp.exp(unif((DEPTH, G_HEADS), math.log(1e-3), math.log(1e-1)))
    a0 = unif((DEPTH, L_WIDTH), 0.9, 0.999) ** (1.0 / LRU_C)
    return {
        'x_prompt': nrm((BATCH, SEQ, D_MODEL), 1.0),
        'x_sample': nrm((DEC_BATCH, DEC_SEQ, D_MODEL), 1.0),
        'cache_k': nrm((DEPTH, n_pool, PAGE_SIZE, A_HEADS, 2 * A_DQK), 1.0),
        'cache_v': nrm((DEPTH, n_pool, PAGE_SIZE, A_HEADS, A_DV), 1.0),
        'page_table': page_table,
        'state_gdn': nrm((DEPTH, DEC_BATCH, G_HEADS, G_DK, G_DV), 0.1),
        'state_gdn_conv': nrm((DEPTH, DEC_BATCH, CONV_W - 1, GDN_CONV_CH), 1.0),
        'state_lru': nrm((DEPTH, DEC_BATCH, L_WIDTH), 0.5),
        'state_lru_conv': nrm((DEPTH, DEC_BATCH, CONV_W - 1, L_WIDTH), 1.0),
        'norm1_g': 1.0 + nrm((DEPTH, D_MODEL), 0.02),
        'norm2_g': 1.0 + nrm((DEPTH, D_MODEL), 0.02),
        'final_g': 1.0 + nrm((D_MODEL,), 0.02),
        'w_in': nrm((DEPTH, D_MODEL, D_IN), D_MODEL ** -0.5),
        'w_gate': nrm((DEPTH, D_MODEL, 3 * D_MODEL), D_MODEL ** -0.5),
        'b_gate': nrm((DEPTH, 3 * D_MODEL), 0.01),
        'g_conv_w': nrm((DEPTH, CONV_W, GDN_CONV_CH), 0.5),
        'g_a_log': jnp.log(unif((DEPTH, G_HEADS), 1.0, 16.0)),
        'g_dt_bias': dt + jnp.log(-jnp.expm1(-dt)),
        'g_norm_g': 1.0 + nrm((DEPTH, G_DV), 0.02),
        'a_lambda': nrm((DEPTH, 4, A_DQK), 0.1),
        'a_subln_g': 1.0 + nrm((DEPTH, A_DV), 0.02),
        'rel_bias': nrm((NUM_BUCKETS, A_HEADS), 0.2),
        'l_conv_w': nrm((DEPTH, CONV_W, L_WIDTH), 0.5),
        'l_conv_b': nrm((DEPTH, L_WIDTH), 0.01),
        'l_wa': nrm((DEPTH, L_BLOCKS, L_BLOCK_DIM, L_BLOCK_DIM), L_BLOCK_DIM ** -0.5),
        'l_ba': nrm((DEPTH, L_WIDTH), 0.01),
        'l_wx': nrm((DEPTH, L_BLOCKS, L_BLOCK_DIM, L_BLOCK_DIM), L_BLOCK_DIM ** -0.5),
        'l_bx': nrm((DEPTH, L_WIDTH), 0.01),
        'l_lambda': jnp.log(a0 / (1.0 - a0)),
        'w_br_g': nrm((DEPTH, G_WV, D_MODEL), G_WV ** -0.5),
        'w_br_a': nrm((DEPTH, A_WV, D_MODEL), A_WV ** -0.5),
        'w_br_l': nrm((DEPTH, L_WIDTH, D_MODEL), L_WIDTH ** -0.5),
        'w_out': nrm((DEPTH, D_MODEL, D_MODEL), D_MODEL ** -0.5),
        'f_w1': nrm((n_dense, D_MODEL, D_FF), D_MODEL ** -0.5),
        'f_w3': nrm((n_dense, D_MODEL, D_FF), D_MODEL ** -0.5),
        'f_w2': nrm((n_dense, D_FF, D_MODEL), D_FF ** -0.5),
        'm_router': nrm((n_moe, D_MODEL, N_EXPERTS), D_MODEL ** -0.5),
        'm_w1': nrm((n_moe, N_EXPERTS, D_MODEL, D_FF_EXPERT), D_MODEL ** -0.5),
        'm_w3': nrm((n_moe, N_EXPERTS, D_MODEL, D_FF_EXPERT), D_MODEL ** -0.5),
        'm_w2': nrm((n_moe, N_EXPERTS, D_FF_EXPERT, D_MODEL), D_FF_EXPERT ** -0.5),
    }


def reference(x_prompt, x_sample, cache_k, cache_v, page_table, state_gdn, state_gdn_conv, state_lru,
              state_lru_conv, norm1_g, norm2_g, final_g, w_in, w_gate, b_gate, g_conv_w, g_a_log, g_dt_bias,
              g_norm_g, a_lambda, a_subln_g, rel_bias, l_conv_w, l_conv_b, l_wa, l_ba, l_wx, l_bx, l_lambda,
              w_br_g, w_br_a, w_br_l, w_out, f_w1, f_w3, f_w2, m_router, m_w1, m_w3, m_w2):
    p = dict(norm1_g=norm1_g, norm2_g=norm2_g, final_g=final_g, w_in=w_in, w_gate=w_gate, b_gate=b_gate,
             g_conv_w=g_conv_w, g_a_log=g_a_log, g_dt_bias=g_dt_bias, g_norm_g=g_norm_g, a_lambda=a_lambda,
             a_subln_g=a_subln_g, rel_bias=rel_bias, l_conv_w=l_conv_w, l_conv_b=l_conv_b, l_wa=l_wa, l_ba=l_ba,
             l_wx=l_wx, l_bx=l_bx, l_lambda=l_lambda, w_br_g=w_br_g, w_br_a=w_br_a, w_br_l=w_br_l, w_out=w_out,
             f_w1=f_w1, f_w3=f_w3, f_w2=f_w2, m_router=m_router, m_w1=m_w1, m_w3=m_w3, m_w2=m_w2)
    bp, tp, _ = x_prompt.shape
    dt = x_prompt.dtype
    z_gdn = jnp.zeros((DEPTH, bp, G_HEADS, G_DK, G_DV), F32)
    z_gconv = jnp.zeros((DEPTH, bp, CONV_W - 1, GDN_CONV_CH), dt)
    z_lru = jnp.zeros((DEPTH, bp, L_WIDTH), F32)
    z_lconv = jnp.zeros((DEPTH, bp, CONV_W - 1, L_WIDTH), dt)
    pos_p = jnp.arange(tp, dtype=jnp.int32)
    y_p, k_p, v_p, sg_p, cg_p, sl_p, cl_p = trunk(x_prompt, pos_p, None, None, None,
                                                  z_gdn, z_gconv, z_lru, z_lconv, p)
    past_len = page_table.shape[1] * cache_k.shape[2]
    pos_s = past_len + jnp.arange(x_sample.shape[1], dtype=jnp.int32)
    y_s, k_s, v_s, sg_s, cg_s, sl_s, cl_s = trunk(x_sample, pos_s, cache_k, cache_v, page_table,
                                                  state_gdn, state_gdn_conv, state_lru, state_lru_conv, p)
    return (y_p, y_s, k_p, v_p, k_s, v_s, sg_p, sg_s, cg_p, cg_s, sl_p, sl_s, cl_p, cl_s)
```

```python
import functools
import math

import jax
import jax.numpy as jnp
from jax import lax
from jax.experimental import pallas as pl
from jax.experimental.pallas import tpu as pltpu

F32 = jnp.float32
BF16 = jnp.bfloat16
EPS = 1e-6
NEG = -1e30

G_HEADS, G_DK, G_DV = 4, 128, 128
G_W = G_HEADS * G_DK
GDN_CHUNK = 64
CONV_W = 4
A_HEADS, A_DQK, A_DV = 4, 64, 128
A_W = A_HEADS * A_DV
L_WIDTH, L_BLOCKS = 512, 8
LRU_C = 8.0
NUM_BUCKETS, MAX_DISTANCE = 32, 128
N_EXPERTS, TOP_K = 8, 2
LANE = 128
SUBLANE = 8
VMEM_LIMIT = 56 * 1024 * 1024

C_GQKV = 0
C_GZ = 3 * G_W
C_AQ = 4 * G_W
C_AK = C_AQ + A_W
C_AV = C_AK + A_W
C_LX = C_AV + A_W
C_LY = C_LX + L_WIDTH
C_GBA = C_LY + L_WIDTH
U_COLS = C_GBA + LANE


def _tile(n, target, mult=SUBLANE):
    best = None
    for d in range(mult, min(n, target) + 1, mult):
        if n % d == 0:
            best = d
    return best if best is not None else n


def _params(*sem):
    return pltpu.CompilerParams(dimension_semantics=sem, vmem_limit_bytes=VMEM_LIMIT)


def _dot(a, b):
    return jnp.dot(a, b, preferred_element_type=F32)


def _dot_nt(a, b):
    return lax.dot_general(a, b, (((1,), (1,)), ((), ())), preferred_element_type=F32)


def _dot_tn(a, b):
    return lax.dot_general(a, b, (((0,), (0,)), ((), ())), preferred_element_type=F32)


def _dot_hi(a, b):
    return jnp.dot(a, b, preferred_element_type=F32, precision=lax.Precision.HIGHEST)


def _dot_nt_hi(a, b):
    return lax.dot_general(a, b, (((1,), (1,)), ((), ())), preferred_element_type=F32,
                           precision=lax.Precision.HIGHEST)


def _sigmoid(x):
    return 1.0 / (1.0 + jnp.exp(-x))


def _silu(x):
    return x * _sigmoid(x)


def _softplus(x):
    return jnp.maximum(x, 0.0) + jnp.log1p(jnp.exp(-jnp.abs(x)))


def _rms(x, g):
    return x * lax.rsqrt(jnp.mean(x * x, axis=-1, keepdims=True) + EPS) * g


def _norm_mm_kernel(x_ref, g_ref, w_ref, b_ref, o_ref, h_scr, *, act):
    @pl.when(pl.program_id(1) == 0)
    def _():
        h_scr[...] = _rms(x_ref[...], g_ref[...]).astype(BF16)

    acc = _dot(h_scr[...], w_ref[...]) + b_ref[...]
    if act == "sigmoid":
        acc = _sigmoid(acc)
    o_ref[...] = acc.astype(o_ref.dtype)


def _norm_matmul(x, g, w, b, act, out_dtype, name):
    m, k = x.shape
    n = w.shape[1]
    tm = _tile(m, 1280)
    tn = _tile(n, 512, LANE)
    return pl.pallas_call(
        functools.partial(_norm_mm_kernel, act=act),
        out_shape=jax.ShapeDtypeStruct((m, n), out_dtype),
        grid=(m // tm, n // tn),
        in_specs=[
            pl.BlockSpec((tm, k), lambda i, j: (i, 0)),
            pl.BlockSpec((1, k), lambda i, j: (0, 0)),
            pl.BlockSpec((k, tn), lambda i, j: (0, j)),
            pl.BlockSpec((1, tn), lambda i, j: (0, j)),
        ],
        out_specs=pl.BlockSpec((tm, tn), lambda i, j: (i, j)),
        scratch_shapes=[pltpu.VMEM((tm, k), BF16)],
        compiler_params=_params("parallel", "arbitrary"),
        name=name,
    )(x, g, w, b)


def _gdn_kernel(qkv_ref, z_ref, gba_ref, cs_ref, s0_ref, cw_ref, ab_ref, ng_ref,
                o_ref, sout_ref, xe, ys, bg, s_scr, *, tt, chunk, rows):
    t = pl.program_id(1)
    nt = pl.num_programs(1)
    w3 = 3 * G_W

    @pl.when(t == 0)
    def _():
        s_scr[...] = s0_ref[...]
        xe[...] = jnp.zeros(xe.shape, F32)
        xe[SUBLANE - (CONV_W - 1):SUBLANE, :] = cs_ref[...]

    xe[SUBLANE:SUBLANE + tt, :] = qkv_ref[...]
    cw = cw_ref[...]
    base = SUBLANE - (CONV_W - 1)
    y = xe[base:base + rows, :] * cw[0:1, :]
    for j in range(1, CONV_W):
        y = y + xe[base + j:base + j + rows, :] * cw[j:j + 1, :]
    xe[0:SUBLANE, :] = xe[tt:tt + SUBLANE, :]

    gba = gba_ref[...]
    if rows > tt:
        gba = jnp.concatenate([gba, jnp.zeros((rows - tt, LANE), F32)], axis=0)
    ab = ab_ref[...]
    beta = _sigmoid(gba)
    gdec = -jnp.exp(ab[0:1, :]) * _softplus(gba + ab[1:2, :])
    y = _silu(y)
    if rows > tt:
        live = lax.broadcasted_iota(jnp.int32, (rows, 1), 0) < tt
        y = jnp.where(live, y, 0.0)
        beta = jnp.where(live, beta, 0.0)
        gdec = jnp.where(live, gdec, 0.0)
    ys[...] = y
    bg[0] = beta
    bg[1] = gdec

    ri = lax.broadcasted_iota(jnp.int32, (chunk, chunk), 0)
    ci = lax.broadcasted_iota(jnp.int32, (chunk, chunk), 1)
    tril = ri >= ci
    strict = ri > ci
    tril_f = tril.astype(F32)
    eye = (ri == ci).astype(F32)
    lane = lax.broadcasted_iota(jnp.int32, (chunk, LANE), 1)
    ng = ng_ref[...]
    n_doubling = int(math.log2(chunk)) - 1

    def do_chunk(c0):
        beta_c = bg[0, pl.ds(c0, chunk), :]
        g_c = bg[1, pl.ds(c0, chunk), :]
        gcs = _dot_hi(tril_f, g_c)
        glast = gcs[chunk - 1:chunk, :]
        for h in range(G_HEADS):
            col = G_HEADS + h
            gi = gcs[:, col:col + 1]
            gj = _dot_nt_hi((lane == col).astype(F32), gcs)
            decay = jnp.where(tril, jnp.exp(gi - gj), 0.0)
            egi = jnp.exp(gi)
            b_h = beta_c[:, h:h + 1]
            q = ys[pl.ds(c0, chunk), h * G_DK:(h + 1) * G_DK]
            k = ys[pl.ds(c0, chunk), G_W + h * G_DK:G_W + (h + 1) * G_DK]
            v = ys[pl.ds(c0, chunk), 2 * G_W + h * G_DV:2 * G_W + (h + 1) * G_DV]
            q = q * lax.rsqrt(jnp.sum(q * q, axis=-1, keepdims=True) + 1e-6) * (G_DK ** -0.5)
            k = k * lax.rsqrt(jnp.sum(k * k, axis=-1, keepdims=True) + 1e-6)
            kb = k * b_h
            k16 = k.astype(BF16)
            lmat = jnp.where(strict, _dot_nt(kb.astype(BF16), k16) * decay, 0.0)
            pw = -lmat
            tinv = eye + pw
            for _ in range(n_doubling):
                pw = _dot_hi(pw, pw)
                tinv = tinv + _dot_hi(tinv, pw)
            t16 = tinv.astype(BF16)
            u = _dot(t16, (v * b_h).astype(BF16))
            wm = _dot(t16, (kb * egi).astype(BF16))
            a_in = _dot_nt(q.astype(BF16), k16) * decay
            s_h = s_scr[h]
            s16 = s_h.astype(BF16)
            v_new = u - _dot(wm.astype(BF16), s16)
            v16 = v_new.astype(BF16)
            o = _dot((q * egi).astype(BF16), s16) + _dot(a_in.astype(BF16), v16)
            gl = glast[:, col:col + 1]
            kd = k * jnp.exp(gl - gi)
            s_scr[h] = s_h * jnp.exp(gl) + _dot_tn(kd.astype(BF16), v16)
            if rows > tt:
                o = o[0:tt, :]
                zz = z_ref[:, h * G_DV:(h + 1) * G_DV]
                o_ref[:, h * G_DV:(h + 1) * G_DV] = (_rms(o, ng) * _silu(zz)).astype(o_ref.dtype)
            else:
                zz = z_ref[pl.ds(c0, chunk), h * G_DV:(h + 1) * G_DV]
                o_ref[pl.ds(c0, chunk), h * G_DV:(h + 1) * G_DV] = (
                    _rms(o, ng) * _silu(zz)).astype(o_ref.dtype)

    n_chunks = rows // chunk
    if n_chunks == 1:
        do_chunk(0)
    else:
        def body(c, carry):
            do_chunk(pl.multiple_of(c * chunk, chunk))
            return carry
        lax.fori_loop(0, n_chunks, body, 0)

    @pl.when(t == nt - 1)
    def _():
        sout_ref[...] = s_scr[...]


def _gdn(u_all, row0, nb, t_len, conv_state, s0, conv_w, ab, norm_g, name):
    chunk = GDN_CHUNK
    if t_len >= chunk:
        tt = _tile(t_len, 256, chunk)
        rows = tt
    else:
        tt = t_len
        rows = chunk
    nt = t_len // tt
    rb0 = row0 // tt
    w3 = 3 * G_W
    return pl.pallas_call(
        functools.partial(_gdn_kernel, tt=tt, chunk=chunk, rows=rows),
        out_shape=(jax.ShapeDtypeStruct((nb * t_len, G_W), BF16),
                   jax.ShapeDtypeStruct((nb, G_HEADS, G_DK, G_DV), F32)),
        grid=(nb, nt),
        in_specs=[
            pl.BlockSpec((tt, w3), lambda b, t: (rb0 + b * nt + t, C_GQKV // w3)),
            pl.BlockSpec((tt, G_W), lambda b, t: (rb0 + b * nt + t, C_GZ // G_W)),
            pl.BlockSpec((tt, LANE), lambda b, t: (rb0 + b * nt + t, C_GBA // LANE)),
            pl.BlockSpec((None, CONV_W - 1, w3), lambda b, t: (b, 0, 0)),
            pl.BlockSpec((None, G_HEADS, G_DK, G_DV), lambda b, t: (b, 0, 0, 0)),
            pl.BlockSpec((CONV_W, w3), lambda b, t: (0, 0)),
            pl.BlockSpec((2, LANE), lambda b, t: (0, 0)),
            pl.BlockSpec((1, G_DV), lambda b, t: (0, 0)),
        ],
        out_specs=(pl.BlockSpec((tt, G_W), lambda b, t: (b * nt + t, 0)),
                   pl.BlockSpec((None, G_HEADS, G_DK, G_DV), lambda b, t: (b, 0, 0, 0))),
        scratch_shapes=[pltpu.VMEM((SUBLANE + rows, w3), F32),
                        pltpu.VMEM((rows, w3), F32),
                        pltpu.VMEM((2, rows, LANE), F32),
                        pltpu.VMEM((G_HEADS, G_DK, G_DV), F32)],
        compiler_params=_params("parallel", "arbitrary"),
        name=name,
    )(u_all, u_all, u_all, conv_state, s0, conv_w, ab, norm_g)


def _gelu_tanh(x):
    return 0.5 * x * (1.0 + jnp.tanh(math.sqrt(2.0 / math.pi) * (x + 0.044715 * (x * x * x))))


def _lru_kernel(lx_ref, ly_ref, cs_ref, h0_ref, cw_ref, cb_ref, wa_ref, ba_ref, wx_ref, bx_ref,
                lam_ref, o_ref, hout_ref, xe, h_scr, *, tt):
    t = pl.program_id(1)
    nt = pl.num_programs(1)

    @pl.when(t == 0)
    def _():
        h_scr[...] = jnp.broadcast_to(h0_ref[...], h_scr.shape)
        xe[0:SUBLANE, :] = jnp.zeros((SUBLANE, L_WIDTH), F32)
        xe[SUBLANE - (CONV_W - 1):SUBLANE, :] = cs_ref[...]

    xe[SUBLANE:SUBLANE + tt, :] = lx_ref[...]
    cw = cw_ref[...]
    base = SUBLANE - (CONV_W - 1)
    xc = xe[base:base + tt, :] * cw[0:1, :]
    for j in range(1, CONV_W):
        xc = xc + xe[base + j:base + j + tt, :] * cw[j:j + 1, :]
    xe[0:SUBLANE, :] = xe[tt:tt + SUBLANE, :]
    xc = xc + cb_ref[...]
    x16 = xc.astype(BF16)
    r = _sigmoid(_dot(x16, wa_ref[...]) + ba_ref[...])
    i = _sigmoid(_dot(x16, wx_ref[...]) + bx_ref[...])
    log_a = -LRU_C * r * _softplus(-lam_ref[...])
    a = jnp.exp(log_a)
    b = jnp.sqrt(jnp.tanh(-log_a) * (a * a + 1.0)) * (i * xc)
    row = lax.broadcasted_iota(jnp.int32, (tt, 1), 0)
    s = 1
    while s < tt:
        keep = row >= s
        a_prev = jnp.where(keep, pltpu.roll(a, s, 0), 1.0)
        b_prev = jnp.where(keep, pltpu.roll(b, s, 0), 0.0)
        b = a * b_prev + b
        a = a * a_prev
        s *= 2
    h = a * h_scr[0:1, :] + b
    h_last = h[tt - 1:tt, :]
    h_scr[...] = jnp.broadcast_to(h_last, h_scr.shape)
    o_ref[...] = (h * _gelu_tanh(ly_ref[...])).astype(o_ref.dtype)

    @pl.when(t == nt - 1)
    def _():
        hout_ref[...] = h_last


def _lru(u_all, row0, nb, t_len, conv_state, h0, cw, cb, wa, ba, wx, bx, lam, name):
    tt = _tile(t_len, 256)
    nt = t_len // tt
    rb0 = row0 // tt
    vec = lambda: pl.BlockSpec((1, L_WIDTH), lambda b, t: (0, 0))
    out, h_last = pl.pallas_call(
        functools.partial(_lru_kernel, tt=tt),
        out_shape=(jax.ShapeDtypeStruct((nb * t_len, L_WIDTH), BF16),
                   jax.ShapeDtypeStruct((nb, 1, L_WIDTH), F32)),
        grid=(nb, nt),
        in_specs=[
            pl.BlockSpec((tt, L_WIDTH), lambda b, t: (rb0 + b * nt + t, C_LX // L_WIDTH)),
            pl.BlockSpec((tt, L_WIDTH), lambda b, t: (rb0 + b * nt + t, C_LY // L_WIDTH)),
            pl.BlockSpec((None, CONV_W - 1, L_WIDTH), lambda b, t: (b, 0, 0)),
            pl.BlockSpec((None, 1, L_WIDTH), lambda b, t: (b, 0, 0)),
            pl.BlockSpec((CONV_W, L_WIDTH), lambda b, t: (0, 0)),
            vec(),
            pl.BlockSpec((L_WIDTH, L_WIDTH), lambda b, t: (0, 0)),
            vec(),
            pl.BlockSpec((L_WIDTH, L_WIDTH), lambda b, t: (0, 0)),
            vec(),
            vec(),
        ],
        out_specs=(pl.BlockSpec((tt, L_WIDTH), lambda b, t: (b * nt + t, 0)),
                   pl.BlockSpec((None, 1, L_WIDTH), lambda b, t: (b, 0, 0))),
        scratch_shapes=[pltpu.VMEM((SUBLANE + tt, L_WIDTH), F32),
                        pltpu.VMEM((SUBLANE, L_WIDTH), F32)],
        compiler_params=_params("parallel", "arbitrary"),
        name=name,
    )(u_all, u_all, conv_state, h0.reshape(nb, 1, L_WIDTH), cw, cb, wa, ba, wx, bx, lam)
    return out, h_last.reshape(nb, L_WIDTH)


def _t5_bucket(dist):
    n = jnp.maximum(dist, 0)
    max_exact = NUM_BUCKETS // 2
    nf = jnp.maximum(n, 1).astype(F32)
    large = max_exact + (jnp.log(nf / max_exact) / math.log(MAX_DISTANCE / max_exact)
                         * (NUM_BUCKETS - max_exact)).astype(jnp.int32)
    large = jnp.minimum(large, NUM_BUCKETS - 1)
    return jnp.where(n < max_exact, n, large)


def _lambda_init(layer):
    return 0.8 - 0.6 * math.exp(-0.3 * layer)


def _diff_lambda(lp, lam_init):
    s1 = jnp.sum(lp[0:1, :] * lp[1:2, :], axis=-1, keepdims=True)
    s2 = jnp.sum(lp[2:3, :] * lp[3:4, :], axis=-1, keepdims=True)
    return jnp.exp(s1) - jnp.exp(s2) + lam_init


def _softmax_step(s, m_ref, l_ref, acc_ref, pv):
    m_prev = m_ref[...]
    m_new = jnp.maximum(m_prev, jnp.max(s, axis=-1, keepdims=True))
    alpha = jnp.exp(m_prev - m_new)
    p = jnp.exp(s - m_new)
    l_ref[...] = alpha * l_ref[...] + jnp.sum(p, axis=-1, keepdims=True)
    acc_ref[...] = alpha * acc_ref[...] + pv(p)
    m_ref[...] = m_new


def _attn_prompt_kernel(rb_ref, q_ref, k_ref, v_ref, lp_ref, g_ref, o_ref,
                        qs, m_scr, l_scr, acc, tab, *, tq, lam_init):
    h = pl.program_id(1)
    qi = pl.program_id(2)
    ki = pl.program_id(3)

    @pl.when((qi == 0) & (ki == 0))
    def _():
        ri = lax.broadcasted_iota(jnp.int32, (tq, tq), 0)
        ci = lax.broadcasted_iota(jnp.int32, (tq, tq), 1)
        far = rb_ref[h * NUM_BUCKETS + NUM_BUCKETS - 1]
        for which in range(2):
            dist = ri - ci + which * tq
            bucket = _t5_bucket(dist)
            val = jnp.full((tq, tq), far, F32)
            for b in range(NUM_BUCKETS - 2, -1, -1):
                val = jnp.where(bucket == b, rb_ref[h * NUM_BUCKETS + b], val)
            if which == 0:
                val = jnp.where(dist >= 0, val, NEG)
            tab[which] = val
        tab[2] = jnp.full((tq, tq), far, F32)

    @pl.when(ki == 0)
    def _():
        q = q_ref[...] * (A_DQK ** -0.5)
        lane = lax.broadcasted_iota(jnp.int32, (tq, 2 * A_DQK), 1)
        qs[0:tq, :] = jnp.where(lane < A_DQK, q, 0.0).astype(BF16)
        qs[tq:2 * tq, :] = jnp.where(lane >= A_DQK, q, 0.0).astype(BF16)
        m_scr[...] = jnp.full(m_scr.shape, NEG, F32)
        l_scr[...] = jnp.zeros(l_scr.shape, F32)
        acc[...] = jnp.zeros(acc.shape, F32)

    @pl.when(ki <= qi)
    def _():
        k16 = k_ref[...].astype(BF16)
        v16 = v_ref[...].astype(BF16)
        bias = tab[jnp.minimum(qi - ki, 2)]
        s = _dot_nt(qs[...], k16)
        s = (s.reshape(2, tq, tq) + bias[None]).reshape(2 * tq, tq)
        _softmax_step(s, m_scr, l_scr, acc, lambda p: _dot(p.astype(BF16), v16))

    @pl.when(ki == qi)
    def _():
        o = acc[...] / l_scr[...]
        lam = _diff_lambda(lp_ref[...], lam_init)
        a = o[0:tq, :] - lam * o[tq:2 * tq, :]
        o_ref[...] = (_rms(a, g_ref[...]) * (1.0 - lam_init)).astype(o_ref.dtype)


def _attn_prompt(u_all, nb, t_len, rel_bias_flat, a_lambda, subln_g, layer, name):
    tq = _tile(t_len, 512, LANE)
    nq = t_len // tq
    lam_init = _lambda_init(layer)
    return pl.pallas_call(
        functools.partial(_attn_prompt_kernel, tq=tq, lam_init=lam_init),
        out_shape=jax.ShapeDtypeStruct((nb * t_len, A_W), BF16),
        grid=(nb, A_HEADS, nq, nq),
        in_specs=[
            pl.BlockSpec(memory_space=pltpu.SMEM),
            pl.BlockSpec((tq, A_DV), lambda b, h, qi, ki: (b * nq + qi, C_AQ // A_DV + h)),
            pl.BlockSpec((tq, A_DV), lambda b, h, qi, ki: (b * nq + jnp.minimum(ki, qi), C_AK // A_DV + h)),
            pl.BlockSpec((tq, A_DV), lambda b, h, qi, ki: (b * nq + jnp.minimum(ki, qi), C_AV // A_DV + h)),
            pl.BlockSpec((4, A_DQK), lambda b, h, qi, ki: (0, 0)),
            pl.BlockSpec((1, A_DV), lambda b, h, qi, ki: (0, 0)),
        ],
        out_specs=pl.BlockSpec((tq, A_DV), lambda b, h, qi, ki: (b * nq + qi, h)),
        scratch_shapes=[pltpu.VMEM((2 * tq, A_DV), BF16),
                        pltpu.VMEM((2 * tq, 1), F32),
                        pltpu.VMEM((2 * tq, 1), F32),
                        pltpu.VMEM((2 * tq, A_DV), F32),
                        pltpu.VMEM((3, tq, tq), F32)],
        compiler_params=_params("arbitrary", "arbitrary", "arbitrary", "arbitrary"),
        name=name,
    )(rel_bias_flat, u_all, u_all, u_all, a_lambda, subln_g)


def _attn_sample_kernel(pt_ref, rb_ref, q_ref, kn_ref, vn_ref, lp_ref, g_ref, *rest,
                        n_pg, t_len, page, past, lam_init):
    kp = rest[0:n_pg]
    vp = rest[n_pg:2 * n_pg]
    o_ref = rest[2 * n_pg]
    qs, m_scr, l_scr, acc, knp, vnp = rest[2 * n_pg + 1:]
    s_id = pl.program_id(1)
    n_steps = pl.num_programs(1)
    n_cache = n_steps - 1
    rows = A_HEADS * 2 * t_len
    cols = page * A_HEADS
    row = lax.broadcasted_iota(jnp.int32, (rows, 1), 0)
    row_h = row // (2 * t_len)
    row_q = row % t_len

    def head_col(b):
        c = jnp.full((rows, 1), rb_ref[b], F32)
        for hh in range(1, A_HEADS):
            c = jnp.where(row_h == hh, rb_ref[hh * NUM_BUCKETS + b], c)
        return c

    def exact_bias(dist):
        bucket = _t5_bucket(dist)
        val = jnp.zeros(dist.shape, F32) + head_col(NUM_BUCKETS - 1)
        for b in range(NUM_BUCKETS - 2, -1, -1):
            val = jnp.where(bucket == b, head_col(b), val)
        return val

    @pl.when(s_id == 0)
    def _():
        q = q_ref[...] * (A_DQK ** -0.5)
        lane = lax.broadcasted_iota(jnp.int32, (t_len, A_DV), 1)
        for hh in range(A_HEADS):
            qh = q[:, hh * A_DV:(hh + 1) * A_DV]
            qs[hh * 2 * t_len:hh * 2 * t_len + t_len, :] = jnp.where(lane < A_DQK, qh, 0.0).astype(BF16)
            qs[hh * 2 * t_len + t_len:(hh + 1) * 2 * t_len, :] = jnp.where(lane >= A_DQK, qh, 0.0).astype(BF16)
        m_scr[...] = jnp.full(m_scr.shape, NEG, F32)
        l_scr[...] = jnp.zeros(l_scr.shape, F32)
        acc[...] = jnp.zeros(acc.shape, F32)

    def cache_step(exact):
        col = lax.broadcasted_iota(jnp.int32, (rows, cols), 1)
        own = (col % A_HEADS) == row_h
        far = head_col(NUM_BUCKETS - 1)
        parts = []
        for i in range(n_pg):
            k16 = kp[i][...].astype(BF16)
            s = _dot_nt(qs[...], k16)
            if exact:
                k_pos = (s_id * n_pg + i) * page + col // A_HEADS
                s = s + exact_bias(past + row_q - k_pos)
            else:
                s = s + far
            parts.append(jnp.where(own, s, NEG))
        s_all = jnp.concatenate(parts, axis=1) if n_pg > 1 else parts[0]

        def pv(p):
            out = None
            for i in range(n_pg):
                c = _dot(p[:, i * cols:(i + 1) * cols].astype(BF16), vp[i][...].astype(BF16))
                out = c if out is None else out + c
            return out

        _softmax_step(s_all, m_scr, l_scr, acc, pv)

    @pl.when(s_id < n_cache - 1)
    def _():
        cache_step(False)

    @pl.when(s_id == n_cache - 1)
    def _():
        cache_step(True)

    @pl.when(s_id == n_steps - 1)
    def _():
        knp[...] = jnp.zeros(knp.shape, BF16)
        vnp[...] = jnp.zeros(vnp.shape, BF16)
        kn = kn_ref[...]
        vn = vn_ref[...]
        for hh in range(A_HEADS):
            knp[hh, 0:t_len, :] = kn[:, hh * A_DV:(hh + 1) * A_DV].astype(BF16)
            vnp[hh, 0:t_len, :] = vn[:, hh * A_DV:(hh + 1) * A_DV].astype(BF16)
        kcol = lax.broadcasted_iota(jnp.int32, (rows, page), 1)
        dist = row_q - kcol
        bias = jnp.where((dist >= 0) & (kcol < t_len), exact_bias(dist), NEG)
        parts = []
        for hh in range(A_HEADS):
            parts.append(_dot_nt(qs[hh * 2 * t_len:(hh + 1) * 2 * t_len, :], knp[hh]))
        s = jnp.concatenate(parts, axis=0) + bias

        def pv(p):
            outs = []
            for hh in range(A_HEADS):
                outs.append(_dot(p[hh * 2 * t_len:(hh + 1) * 2 * t_len, :].astype(BF16), vnp[hh]))
            return jnp.concatenate(outs, axis=0)

        _softmax_step(s, m_scr, l_scr, acc, pv)
        o = acc[...] / l_scr[...]
        lam = _diff_lambda(lp_ref[...], lam_init)
        g = g_ref[...]
        for hh in range(A_HEADS):
            r0 = hh * 2 * t_len
            a = o[r0:r0 + t_len, :] - lam * o[r0 + t_len:r0 + 2 * t_len, :]
            o_ref[:, hh * A_DV:(hh + 1) * A_DV] = (_rms(a, g) * (1.0 - lam_init)).astype(o_ref.dtype)


def _attn_sample(u_all, row0, nb, t_len, cache_k, cache_v, page_table, rel_bias_flat, a_lambda,
                 subln_g, layer, name):
    depth, n_pool, page = cache_k.shape[0], cache_k.shape[1], cache_k.shape[2]
    n_pages = page_table.shape[1]
    n_pg = _tile(n_pages, 8, 1)
    n_cache = n_pages // n_pg
    past = n_pages * page
    rb0 = row0 // t_len
    rows = A_HEADS * 2 * t_len
    ck = cache_k.reshape(depth, n_pool, page * A_HEADS, A_DV)
    cv = cache_v.reshape(depth, n_pool, page * A_HEADS, A_DV)
    pt = page_table.reshape(-1)

    def page_spec(i):
        def imap(b, s, pt_ref):
            step = jnp.minimum(s, n_cache - 1)
            return (layer, pt_ref[b * n_pages + step * n_pg + i], 0, 0)
        return pl.BlockSpec((None, None, page * A_HEADS, A_DV), imap)

    grid_spec = pltpu.PrefetchScalarGridSpec(
        num_scalar_prefetch=1,
        grid=(nb, n_cache + 1),
        in_specs=[
            pl.BlockSpec(memory_space=pltpu.SMEM),
            pl.BlockSpec((t_len, A_W), lambda b, s, pt_ref: (rb0 + b, C_AQ // A_W)),
            pl.BlockSpec((t_len, A_W), lambda b, s, pt_ref: (rb0 + b, C_AK // A_W)),
            pl.BlockSpec((t_len, A_W), lambda b, s, pt_ref: (rb0 + b, C_AV // A_W)),
            pl.BlockSpec((4, A_DQK), lambda b, s, pt_ref: (0, 0)),
            pl.BlockSpec((1, A_DV), lambda b, s, pt_ref: (0, 0)),
        ] + [page_spec(i) for i in range(n_pg)] + [page_spec(i) for i in range(n_pg)],
        out_specs=pl.BlockSpec((t_len, A_W), lambda b, s, pt_ref: (b, 0)),
        scratch_shapes=[pltpu.VMEM((rows, A_DV), BF16),
                        pltpu.VMEM((rows, 1), F32),
                        pltpu.VMEM((rows, 1), F32),
                        pltpu.VMEM((rows, A_DV), F32),
                        pltpu.VMEM((A_HEADS, page, A_DV), BF16),
                        pltpu.VMEM((A_HEADS, page, A_DV), BF16)],
    )
    return pl.pallas_call(
        functools.partial(_attn_sample_kernel, n_pg=n_pg, t_len=t_len, page=page, past=past,
                          lam_init=_lambda_init(layer)),
        out_shape=jax.ShapeDtypeStruct((nb * t_len, A_W), BF16),
        grid_spec=grid_spec,
        compiler_params=_params("arbitrary", "arbitrary"),
        name=name,
    )(pt, rel_bias_flat, u_all, u_all, u_all, a_lambda, subln_g, *([ck] * n_pg), *([cv] * n_pg))


def _merge_kernel(x_ref, og_ref, oa_ref, ol_ref, gt_ref, wg_ref, wa_ref, wl_ref, wo_ref, o_ref):
    d = x_ref.shape[1]
    gt = gt_ref[...].astype(F32)
    m = gt[:, 0:d] * _dot(og_ref[...], wg_ref[...])
    m = m + gt[:, d:2 * d] * _dot(oa_ref[...], wa_ref[...])
    m = m + gt[:, 2 * d:3 * d] * _dot(ol_ref[...], wl_ref[...])
    o_ref[...] = x_ref[...] + _dot(m.astype(BF16), wo_ref[...])


def _merge(x, o_g, o_a, o_l, gates, wg, wa, wl, wo, name):
    m, d = x.shape
    tm = _tile(m, 640)
    row = lambda w: pl.BlockSpec((tm, w), lambda i: (i, 0))
    full = lambda a: pl.BlockSpec(a.shape, lambda i: (0, 0))
    return pl.pallas_call(
        _merge_kernel,
        out_shape=jax.ShapeDtypeStruct((m, d), F32),
        grid=(m // tm,),
        in_specs=[row(d), row(G_W), row(A_W), row(L_WIDTH), row(3 * d),
                  full(wg), full(wa), full(wl), full(wo)],
        out_specs=row(d),
        compiler_params=_params("parallel"),
        name=name,
    )(x, o_g, o_a, o_l, gates, wg, wa, wl, wo)


def _ffn_kernel(x_ref, g_ref, w1_ref, w3_ref, w2_ref, o_ref, h_scr, acc):
    f = pl.program_id(1)

    @pl.when(f == 0)
    def _():
        x = x_ref[...]
        h_scr[...] = _rms(x, g_ref[...]).astype(BF16)
        acc[...] = x

    h = h_scr[...]
    t = _silu(_dot(h, w1_ref[...])) * _dot(h, w3_ref[...])
    acc[...] += _dot(t.astype(BF16), w2_ref[...])

    @pl.when(f == pl.num_programs(1) - 1)
    def _():
        o_ref[...] = acc[...]


def _ffn(x, g, w1, w3, w2, name):
    m, d = x.shape
    ff = w1.shape[1]
    tm = _tile(m, 1280)
    tf = _tile(ff, 512, LANE)
    return pl.pallas_call(
        _ffn_kernel,
        out_shape=jax.ShapeDtypeStruct((m, d), F32),
        grid=(m // tm, ff // tf),
        in_specs=[
            pl.BlockSpec((tm, d), lambda i, f: (i, 0)),
            pl.BlockSpec((1, d), lambda i, f: (0, 0)),
            pl.BlockSpec((d, tf), lambda i, f: (0, f)),
            pl.BlockSpec((d, tf), lambda i, f: (0, f)),
            pl.BlockSpec((tf, d), lambda i, f: (f, 0)),
        ],
        out_specs=pl.BlockSpec((tm, d), lambda i, f: (i, 0)),
        scratch_shapes=[pltpu.VMEM((tm, d), BF16), pltpu.VMEM((tm, d), F32)],
        compiler_params=_params("parallel", "arbitrary"),
        name=name,
    )(x, g, w1, w3, w2)


def _router_kernel(x_ref, g_ref, wr_ref, o_ref):
    h = _rms(x_ref[...], g_ref[...])
    logits = _dot_hi(h, wr_ref[...])
    lane = lax.broadcasted_iota(jnp.int32, logits.shape, 1)
    logits = jnp.where(lane < N_EXPERTS, logits, -jnp.inf)
    m1 = jnp.max(logits, axis=-1, keepdims=True)
    i1 = jnp.min(jnp.where(logits == m1, lane, LANE), axis=-1, keepdims=True)
    rest = jnp.where(lane == i1, -jnp.inf, logits)
    m2 = jnp.max(rest, axis=-1, keepdims=True)
    i2 = jnp.min(jnp.where(rest == m2, lane, LANE), axis=-1, keepdims=True)
    e2 = jnp.exp(m2 - m1)
    den = 1.0 + e2
    o_ref[...] = jnp.where(lane == i1, 1.0 / den, 0.0) + jnp.where(lane == i2, e2 / den, 0.0)


def _router(x, g, wr, name):
    m, d = x.shape
    tm = _tile(m, 1280)
    return pl.pallas_call(
        _router_kernel,
        out_shape=jax.ShapeDtypeStruct((m, LANE), F32),
        grid=(m // tm,),
        in_specs=[pl.BlockSpec((tm, d), lambda i: (i, 0)),
                  pl.BlockSpec((1, d), lambda i: (0, 0)),
                  pl.BlockSpec((d, LANE), lambda i: (0, 0))],
        out_specs=pl.BlockSpec((tm, LANE), lambda i: (i, 0)),
        compiler_params=_params("parallel"),
        name=name,
    )(x, g, wr)


def _moe_kernel(x_ref, g_ref, gate_ref, w1_ref, w3_ref, w2_ref, fg_ref, o_ref, h_scr, acc):
    e = pl.program_id(1)
    f = pl.program_id(2)
    first = (e == 0) & (f == 0)
    last = (e == pl.num_programs(1) - 1) & (f == pl.num_programs(2) - 1)

    @pl.when(first)
    def _():
        x = x_ref[...]
        h_scr[...] = _rms(x, g_ref[...]).astype(BF16)
        acc[...] = x

    gate = gate_ref[...]
    lane = lax.broadcasted_iota(jnp.int32, gate.shape, 1)
    ge = jnp.sum(jnp.where(lane == e, gate, 0.0), axis=-1, keepdims=True)
    h = h_scr[...]
    t = _silu(_dot(h, w1_ref[...])) * _dot(h, w3_ref[...])
    acc[...] += ge * _dot(t.astype(BF16), w2_ref[...])

    @pl.when(last)
    def _():
        o_ref[...] = _rms(acc[...], fg_ref[...])


def _moe(x, g, gate, w1, w3, w2, final_g, name):
    m, d = x.shape
    ne, _, ff = w1.shape
    tm = _tile(m, 1280)
    tf = _tile(ff, 512, LANE)
    return pl.pallas_call(
        _moe_kernel,
        out_shape=jax.ShapeDtypeStruct((m, d), F32),
        grid=(m // tm, ne, ff // tf),
        in_specs=[
            pl.BlockSpec((tm, d), lambda i, e, f: (i, 0)),
            pl.BlockSpec((1, d), lambda i, e, f: (0, 0)),
            pl.BlockSpec((tm, LANE), lambda i, e, f: (i, 0)),
            pl.BlockSpec((None, d, tf), lambda i, e, f: (e, 0, f)),
            pl.BlockSpec((None, d, tf), lambda i, e, f: (e, 0, f)),
            pl.BlockSpec((None, tf, d), lambda i, e, f: (e, f, 0)),
            pl.BlockSpec((1, d), lambda i, e, f: (0, 0)),
        ],
        out_specs=pl.BlockSpec((tm, d), lambda i, e, f: (i, 0)),
        scratch_shapes=[pltpu.VMEM((tm, d), BF16), pltpu.VMEM((tm, d), F32)],
        compiler_params=_params("parallel", "arbitrary", "arbitrary"),
        name=name,
    )(x, g, gate, w1, w3, w2, final_g)


def _final_norm_kernel(x_ref, g_ref, o_ref):
    o_ref[...] = _rms(x_ref[...], g_ref[...])


def _final_norm(x, g, name):
    m, d = x.shape
    tm = _tile(m, 1280)
    return pl.pallas_call(
        _final_norm_kernel,
        out_shape=jax.ShapeDtypeStruct((m, d), F32),
        grid=(m // tm,),
        in_specs=[pl.BlockSpec((tm, d), lambda i: (i, 0)), pl.BlockSpec((1, d), lambda i: (0, 0))],
        out_specs=pl.BlockSpec((tm, d), lambda i: (i, 0)),
        compiler_params=_params("parallel"),
        name=name,
    )(x, g)


def _reorder_w_in(w):
    o_gb = 4 * G_W
    o_aq = o_gb + 2 * G_HEADS
    pad = jnp.zeros((w.shape[0], LANE - 2 * G_HEADS), w.dtype)
    return jnp.concatenate([w[:, :o_gb], w[:, o_aq:], w[:, o_gb:o_aq], pad], axis=1)


def _block_diag(w):
    n, d, _ = w.shape
    eye = jnp.eye(n, dtype=w.dtype)
    return (eye[:, None, :, None] * w[:, :, None, :]).reshape(n * d, n * d)


def kernel(x_prompt, x_sample, cache_k, cache_v, page_table, state_gdn, state_gdn_conv, state_lru, state_lru_conv, norm1_g, norm2_g, final_g, w_in, w_gate, b_gate, g_conv_w, g_a_log, g_dt_bias, g_norm_g, a_lambda, a_subln_g, rel_bias, l_conv_w, l_conv_b, l_wa, l_ba, l_wx, l_bx, l_lambda, w_br_g, w_br_a, w_br_l, w_out, f_w1, f_w3, f_w2, m_router, m_w1, m_w3, m_w2):
    bp, tp, d = x_prompt.shape
    bs, ts, _ = x_sample.shape
    depth = w_in.shape[0]
    mp = bp * tp
    ms = bs * ts
    x = jnp.concatenate([x_prompt.reshape(mp, d), x_sample.reshape(ms, d)], axis=0)
    rb_flat = rel_bias.T.reshape(-1)
    zeros_u = jnp.zeros((1, U_COLS), F32)
    row = lambda v: v.reshape(1, -1)

    k_p, v_p, k_s, v_s = [], [], [], []
    sg_p, sg_s, cg_p, cg_s, sl_p, sl_s, cl_p, cl_s = [], [], [], [], [], [], [], []
    y = None
    for l in range(depth):
        u = _norm_matmul(x, row(norm1_g[l]), _reorder_w_in(w_in[l]).astype(BF16), zeros_u,
                         None, F32, f"in_proj_{l}")
        gates = _norm_matmul(x, row(norm1_g[l]), w_gate[l].astype(BF16), row(b_gate[l]),
                             "sigmoid", BF16, f"gate_proj_{l}")

        ab = jnp.zeros((2, LANE), F32)
        ab = ab.at[0, G_HEADS:2 * G_HEADS].set(g_a_log[l]).at[1, G_HEADS:2 * G_HEADS].set(g_dt_bias[l])
        ng = row(g_norm_g[l])
        og_p, s_p = _gdn(u, 0, bp, tp, jnp.zeros((bp, CONV_W - 1, 3 * G_W), F32),
                         jnp.zeros((bp, G_HEADS, G_DK, G_DV), F32), g_conv_w[l], ab, ng, f"gdn_prompt_{l}")
        og_s, s_s = _gdn(u, mp, bs, ts, state_gdn_conv[l], state_gdn[l], g_conv_w[l], ab, ng,
                         f"gdn_sample_{l}")

        sub_g = row(a_subln_g[l])
        oa_p = _attn_prompt(u, bp, tp, rb_flat, a_lambda[l], sub_g, l, f"attn_prompt_{l}")
        oa_s = _attn_sample(u, mp, bs, ts, cache_k, cache_v, page_table, rb_flat, a_lambda[l], sub_g, l,
                            f"attn_sample_{l}")

        wa_bd = _block_diag(l_wa[l]).astype(BF16)
        wx_bd = _block_diag(l_wx[l]).astype(BF16)
        lru_args = (l_conv_w[l], row(l_conv_b[l]), wa_bd, row(l_ba[l]), wx_bd, row(l_bx[l]), row(l_lambda[l]))
        ol_p, h_p = _lru(u, 0, bp, tp, jnp.zeros((bp, CONV_W - 1, L_WIDTH), F32),
                         jnp.zeros((bp, L_WIDTH), F32), *lru_args, f"lru_prompt_{l}")
        ol_s, h_s = _lru(u, mp, bs, ts, state_lru_conv[l], state_lru[l], *lru_args, f"lru_sample_{l}")

        x = _merge(x, jnp.concatenate([og_p, og_s]), jnp.concatenate([oa_p, oa_s]),
                   jnp.concatenate([ol_p, ol_s]), gates, w_br_g[l].astype(BF16), w_br_a[l].astype(BF16),
                   w_br_l[l].astype(BF16), w_out[l].astype(BF16), f"merge_{l}")

        j = l // 2
        if l % 2 == 0:
            x = _ffn(x, row(norm2_g[l]), f_w1[j].astype(BF16), f_w3[j].astype(BF16), f_w2[j].astype(BF16),
                     f"ffn_{l}")
            if l == depth - 1:
                y = _final_norm(x, row(final_g), "final_norm")
        else:
            wr = jnp.zeros((d, LANE), F32).at[:, :N_EXPERTS].set(m_router[j])
            gate = _router(x, row(norm2_g[l]), wr, f"router_{l}")
            fg = row(final_g) if l == depth - 1 else None
            if fg is None:
                raise NotImplementedError("mixture layer is only fused with the final norm")
            y = _moe(x, row(norm2_g[l]), gate, m_w1[j].astype(BF16), m_w3[j].astype(BF16),
                     m_w2[j].astype(BF16), fg, f"moe_{l}")

        up = u[:mp]
        us = u[mp:]
        k_p.append(up[:, C_AK:C_AK + A_W].reshape(bp, tp, A_HEADS, A_DV))
        v_p.append(up[:, C_AV:C_AV + A_W].reshape(bp, tp, A_HEADS, A_DV))
        k_s.append(us[:, C_AK:C_AK + A_W].reshape(bs, ts, A_HEADS, A_DV))
        v_s.append(us[:, C_AV:C_AV + A_W].reshape(bs, ts, A_HEADS, A_DV))
        sg_p.append(s_p)
        sg_s.append(s_s)
        cg_p.append(up.reshape(bp, tp, U_COLS)[:, tp - (CONV_W - 1):, :3 * G_W])
        cg_s.append(us.reshape(bs, ts, U_COLS)[:, ts - (CONV_W - 1):, :3 * G_W])
        sl_p.append(h_p)
        sl_s.append(h_s)
        cl_p.append(up.reshape(bp, tp, U_COLS)[:, tp - (CONV_W - 1):, C_LX:C_LX + L_WIDTH])
        cl_s.append(us.reshape(bs, ts, U_COLS)[:, ts - (CONV_W - 1):, C_LX:C_LX + L_WIDTH])

    st = jnp.stack
    return (y[:mp].reshape(bp, tp, d), y[mp:].reshape(bs, ts, d),
            st(k_p), st(v_p), st(k_s), st(v_s), st(sg_p), st(sg_s), st(cg_p), st(cg_s),
            st(sl_p), st(sl_s), st(cl_p), st(cl_s))
```

```python
import functools
import math

import jax
import jax.numpy as jnp
from jax import lax
from jax.experimental import pallas as pl
from jax.experimental.pallas import tpu as pltpu

F32 = jnp.float32
BF16 = jnp.bfloat16
EPS = 1e-6
NEG = -1e30

G_HEADS, G_DK, G_DV = 4, 128, 128
G_W = G_HEADS * G_DK
GDN_CHUNK = 64
CONV_W = 4
A_HEADS, A_DQK, A_DV = 4, 64, 128
A_W = A_HEADS * A_DV
L_WIDTH, L_BLOCKS = 512, 8
LRU_C = 8.0
NUM_BUCKETS, MAX_DISTANCE = 32, 128
N_EXPERTS, TOP_K = 8, 2
LANE = 128
SUBLANE = 8
VMEM_LIMIT = 56 * 1024 * 1024

C_GQKV = 0
C_GZ = 3 * G_W
C_AQ = 4 * G_W
C_AK = C_AQ + A_W
C_AV = C_AK + A_W
C_LX = C_AV + A_W
C_LY = C_LX + L_WIDTH
U_COLS = C_LY + L_WIDTH


def _tile(n, target, mult=SUBLANE):
    best = None
    for d in range(mult, min(n, target) + 1, mult):
        if n % d == 0:
            best = d
    return best if best is not None else n


def _params(*sem):
    return pltpu.CompilerParams(dimension_semantics=sem, vmem_limit_bytes=VMEM_LIMIT)


def _dot(a, b):
    return jnp.dot(a, b, preferred_element_type=F32)


def _dot_nt(a, b):
    return lax.dot_general(a, b, (((1,), (1,)), ((), ())), preferred_element_type=F32)


def _dot_tn(a, b):
    return lax.dot_general(a, b, (((0,), (0,)), ((), ())), preferred_element_type=F32)


def _split(x):
    hi = x.astype(BF16)
    return hi, (x - hi.astype(F32)).astype(BF16)


def _dot_split(a, b_hi, b_lo):
    a_hi, a_lo = _split(a)
    return _dot(a_hi, b_hi) + (_dot(a_hi, b_lo) + _dot(a_lo, b_hi))


def _dot_lhs_exact(a16, b):
    b_hi, b_lo = _split(b)
    return _dot(a16, b_hi) + _dot(a16, b_lo)


def _sigmoid(x):
    return 1.0 / (1.0 + jnp.exp(-x))


def _silu(x):
    return x * _sigmoid(x)


def _softplus(x):
    return jnp.maximum(x, 0.0) + jnp.log1p(jnp.exp(-jnp.abs(x)))


def _rms(x, g):
    return x * lax.rsqrt(jnp.mean(x * x, axis=-1, keepdims=True) + EPS) * g


def _norm_mm_kernel(x_ref, g_ref, w_ref, b_ref, o_ref, h_scr, *, act):
    @pl.when(pl.program_id(1) == 0)
    def _():
        h_scr[...] = _rms(x_ref[...], g_ref[...]).astype(BF16)

    acc = _dot(h_scr[...], w_ref[...]) + b_ref[...]
    if act == "sigmoid":
        acc = _sigmoid(acc)
    o_ref[...] = acc.astype(o_ref.dtype)


def _norm_matmul(x, g, w, b, act, out_dtype, name):
    m, k = x.shape
    n = w.shape[1]
    tm = _tile(m, 1280)
    tn = _tile(n, 512, LANE)
    return pl.pallas_call(
        functools.partial(_norm_mm_kernel, act=act),
        out_shape=jax.ShapeDtypeStruct((m, n), out_dtype),
        grid=(m // tm, n // tn),
        in_specs=[
            pl.BlockSpec((tm, k), lambda i, j: (i, 0)),
            pl.BlockSpec((1, k), lambda i, j: (0, 0)),
            pl.BlockSpec((k, tn), lambda i, j: (0, j)),
            pl.BlockSpec((1, tn), lambda i, j: (0, j)),
        ],
        out_specs=pl.BlockSpec((tm, tn), lambda i, j: (i, j)),
        scratch_shapes=[pltpu.VMEM((tm, k), BF16)],
        compiler_params=_params("parallel", "arbitrary"),
        name=name,
    )(x, g, w, b)


def _gdn_kernel(qkv_ref, z_ref, gba_ref, cs_ref, s0_ref, cw_ref, ab_ref, ng_ref,
                o_ref, sout_ref, xe, ys, bg, s_scr, *, tt, chunk, rows):
    t = pl.program_id(1)
    nt = pl.num_programs(1)
    hc = G_HEADS * chunk

    @pl.when(t == 0)
    def _():
        for h in range(G_HEADS):
            s_scr[:, h * G_DV:(h + 1) * G_DV] = s0_ref[h]
        xe[...] = jnp.zeros(xe.shape, F32)
        xe[SUBLANE - (CONV_W - 1):SUBLANE, :] = cs_ref[...]

    xe[SUBLANE:SUBLANE + tt, :] = qkv_ref[...]
    cw = cw_ref[...]
    base = SUBLANE - (CONV_W - 1)
    y = xe[base:base + rows, :] * cw[0:1, :]
    for j in range(1, CONV_W):
        y = y + xe[base + j:base + j + rows, :] * cw[j:j + 1, :]
    xe[0:SUBLANE, :] = xe[tt:tt + SUBLANE, :]

    gba = gba_ref[...]
    if rows > tt:
        gba = jnp.concatenate([gba, jnp.zeros((rows - tt, LANE), F32)], axis=0)
    ab = ab_ref[...]
    lane_g = lax.broadcasted_iota(jnp.int32, (rows, LANE), 1)
    bgc = jnp.where(lane_g < G_HEADS, _sigmoid(gba), -jnp.exp(ab[0:1, :]) * _softplus(gba + ab[1:2, :]))
    y = _silu(y)
    if rows > tt:
        live = lax.broadcasted_iota(jnp.int32, (rows, 1), 0) < tt
        y = jnp.where(live, y, 0.0)
        bgc = jnp.where(live, bgc, 0.0)
    bg[...] = bgc
    for h in range(G_HEADS):
        q = y[:, h * G_DK:(h + 1) * G_DK]
        k = y[:, G_W + h * G_DK:G_W + (h + 1) * G_DK]
        ys[:, h * G_DK:(h + 1) * G_DK] = q * lax.rsqrt(jnp.sum(q * q, axis=-1, keepdims=True) + 1e-6) * (G_DK ** -0.5)
        ys[:, G_W + h * G_DK:G_W + (h + 1) * G_DK] = k * lax.rsqrt(jnp.sum(k * k, axis=-1, keepdims=True) + 1e-6)
    ys[:, 2 * G_W:3 * G_W] = y[:, 2 * G_W:3 * G_W]

    def iota(shape, d):
        return lax.broadcasted_iota(jnp.int32, shape, d)

    r_c = iota((chunk, hc), 0)
    j_c = iota((chunk, hc), 1) % chunk
    tril_cat = r_c >= j_c
    strict_cat = r_c > j_c
    eye_cat = (r_c == j_c).astype(F32)
    triu_cat = (r_c <= j_c).astype(F32)
    tril_f = jnp.where(iota((chunk, chunk), 0) >= iota((chunk, chunk), 1), 1.0, 0.0).astype(BF16)
    ones_cc = jnp.ones((chunk, chunk), BF16)
    l_s = iota((LANE, hc + 2 * G_W), 0)
    n_s = iota((LANE, hc + 2 * G_W), 1)
    src_lane = jnp.where(n_s < hc, G_HEADS + n_s // chunk,
                         jnp.where(n_s < hc + G_W, G_HEADS + (n_s - hc) // G_DK,
                                   (n_s - hc - G_W) // G_DK))
    sel = jnp.where(l_s == src_lane, 1.0, 0.0).astype(BF16)
    bd_cc = iota((hc, hc), 0) // chunk == iota((hc, hc), 1) // chunk
    bd_cd = iota((hc, G_W), 0) // chunk == iota((hc, G_W), 1) // G_DK
    bd_dd = iota((G_W, G_W), 0) // G_DK == iota((G_W, G_W), 1) // G_DK
    ng = ng_ref[...]
    n_doubling = int(math.log2(chunk)) - 1

    def blockdiag(x, mask):
        reps = mask.shape[0] // x.shape[0]
        return jnp.where(mask, jnp.concatenate([x] * reps, axis=0), 0.0)

    def phase_a(c0):
        bg_hi, bg_lo = _split(bg[c0:c0 + chunk, :])
        selm = _dot(bg_hi, sel) + _dot(bg_lo, sel)
        cum = _dot_lhs_exact(tril_f, selm[:, 0:hc + G_W])
        gi_cat = cum[:, 0:hc]
        gi = cum[:, hc:hc + G_W]
        gj_cat = _dot_lhs_exact(ones_cc, selm[:, 0:hc] * triu_cat)
        beta = selm[:, hc + G_W:hc + 2 * G_W]
        decay = jnp.where(tril_cat, jnp.exp(gi_cat - gj_cat), 0.0)
        egi = jnp.exp(gi)
        glast = gi[chunk - 1:chunk, :]
        q = ys[c0:c0 + chunk, 0:G_W]
        k = ys[c0:c0 + chunk, G_W:2 * G_W]
        v = ys[c0:c0 + chunk, 2 * G_W:3 * G_W]
        kb = k * beta
        bdk = blockdiag(k, bd_cd).astype(BF16)
        la = _dot_nt(jnp.concatenate([kb, q], axis=0).astype(BF16), bdk)
        lmat = jnp.where(strict_cat, la[0:chunk] * decay, 0.0)
        a_in = la[chunk:2 * chunk] * decay
        pw = -lmat
        tinv = eye_cat + pw
        bd_hi, bd_lo = _split(blockdiag(pw, bd_cc))
        for _ in range(n_doubling):
            pw = _dot_split(pw, bd_hi, bd_lo)
            bd_hi, bd_lo = _split(blockdiag(pw, bd_cc))
            tinv = tinv + _dot_split(tinv, bd_hi, bd_lo)
        rhs = jnp.concatenate([blockdiag(v * beta, bd_cd), blockdiag(kb * egi, bd_cd)], axis=1)
        uw = _dot(tinv.astype(BF16), rhs.astype(BF16))
        return dict(u=uw[:, 0:G_W], w=uw[:, G_W:2 * G_W], qe=q * egi, a=a_in.astype(BF16),
                    kd=(k * jnp.exp(glast - gi)).astype(BF16), gl=jnp.exp(glast))

    def phase_b(c0, a):
        s_cat = s_scr[...]
        bds = blockdiag(s_cat, bd_dd).astype(BF16)
        ws = _dot(jnp.concatenate([a["w"], a["qe"]], axis=0).astype(BF16), bds)
        v_new = a["u"] - ws[0:chunk]
        o = ws[chunk:2 * chunk] + _dot(a["a"], blockdiag(v_new, bd_cd).astype(BF16))
        v16 = v_new.astype(BF16)
        n_out = min(tt, chunk)
        for h in range(G_HEADS):
            hs = slice(h * G_DV, (h + 1) * G_DV)
            s_scr[:, hs] = s_cat[:, hs] * a["gl"][:, hs] + _dot_tn(a["kd"][:, hs], v16[:, hs])
            zz = z_ref[c0:c0 + n_out, hs]
            o_ref[c0:c0 + n_out, hs] = (_rms(o[0:n_out, hs], ng) * _silu(zz)).astype(o_ref.dtype)

    n_chunks = rows // chunk
    nxt = phase_a(0)
    for c in range(n_chunks):
        cur = nxt
        if c + 1 < n_chunks:
            nxt = phase_a((c + 1) * chunk)
        phase_b(c * chunk, cur)

    @pl.when(t == nt - 1)
    def _():
        for h in range(G_HEADS):
            sout_ref[h] = s_scr[:, h * G_DV:(h + 1) * G_DV]


def _gdn(u_all, gba_all, row0, nb, t_len, conv_state, s0, conv_w, ab, norm_g, name):
    chunk = GDN_CHUNK
    if t_len >= chunk:
        tt = _tile(t_len, 256, chunk)
        rows = tt
    else:
        tt = t_len
        rows = chunk
    nt = t_len // tt
    rb0 = row0 // tt
    w3 = 3 * G_W
    return pl.pallas_call(
        functools.partial(_gdn_kernel, tt=tt, chunk=chunk, rows=rows),
        out_shape=(jax.ShapeDtypeStruct((nb * t_len, G_W), BF16),
                   jax.ShapeDtypeStruct((nb, G_HEADS, G_DK, G_DV), F32)),
        grid=(nb, nt),
        in_specs=[
            pl.BlockSpec((tt, w3), lambda b, t: (rb0 + b * nt + t, C_GQKV // w3)),
            pl.BlockSpec((tt, G_W), lambda b, t: (rb0 + b * nt + t, C_GZ // G_W)),
            pl.BlockSpec((tt, LANE), lambda b, t: (rb0 + b * nt + t, 0)),
            pl.BlockSpec((None, CONV_W - 1, w3), lambda b, t: (b, 0, 0)),
            pl.BlockSpec((None, G_HEADS, G_DK, G_DV), lambda b, t: (b, 0, 0, 0)),
            pl.BlockSpec((CONV_W, w3), lambda b, t: (0, 0)),
            pl.BlockSpec((2, LANE), lambda b, t: (0, 0)),
            pl.BlockSpec((1, G_DV), lambda b, t: (0, 0)),
        ],
        out_specs=(pl.BlockSpec((tt, G_W), lambda b, t: (b * nt + t, 0)),
                   pl.BlockSpec((None, G_HEADS, G_DK, G_DV), lambda b, t: (b, 0, 0, 0))),
        scratch_shapes=[pltpu.VMEM((SUBLANE + rows, w3), F32),
                        pltpu.VMEM((rows, w3), F32),
                        pltpu.VMEM((rows, LANE), F32),
                        pltpu.VMEM((G_DK, G_HEADS * G_DV), F32)],
        compiler_params=_params("parallel", "arbitrary"),
        name=name,
    )(u_all, u_all, gba_all, conv_state, s0, conv_w, ab, norm_g)


def _gelu_tanh(x):
    return 0.5 * x * (1.0 + jnp.tanh(math.sqrt(2.0 / math.pi) * (x + 0.044715 * (x * x * x))))


def _lru_kernel(lx_ref, ly_ref, cs_ref, h0_ref, cw_ref, cb_ref, wa_ref, ba_ref, wx_ref, bx_ref,
                lam_ref, o_ref, hout_ref, xe, h_scr, *, tt):
    t = pl.program_id(1)
    nt = pl.num_programs(1)

    @pl.when(t == 0)
    def _():
        h_scr[...] = jnp.broadcast_to(h0_ref[...], h_scr.shape)
        xe[0:SUBLANE, :] = jnp.zeros((SUBLANE, L_WIDTH), F32)
        xe[SUBLANE - (CONV_W - 1):SUBLANE, :] = cs_ref[...]

    xe[SUBLANE:SUBLANE + tt, :] = lx_ref[...]
    cw = cw_ref[...]
    base = SUBLANE - (CONV_W - 1)
    xc = xe[base:base + tt, :] * cw[0:1, :]
    for j in range(1, CONV_W):
        xc = xc + xe[base + j:base + j + tt, :] * cw[j:j + 1, :]
    xe[0:SUBLANE, :] = xe[tt:tt + SUBLANE, :]
    xc = xc + cb_ref[...]
    x16 = xc.astype(BF16)
    r = _sigmoid(_dot(x16, wa_ref[...]) + ba_ref[...])
    i = _sigmoid(_dot(x16, wx_ref[...]) + bx_ref[...])
    log_a = -LRU_C * r * _softplus(-lam_ref[...])
    a = jnp.exp(log_a)
    b = jnp.sqrt(jnp.tanh(-log_a) * (a * a + 1.0)) * (i * xc)
    row = lax.broadcasted_iota(jnp.int32, (tt, 1), 0)
    s = 1
    while s < tt:
        keep = row >= s
        a_prev = jnp.where(keep, pltpu.roll(a, s, 0), 1.0)
        b_prev = jnp.where(keep, pltpu.roll(b, s, 0), 0.0)
        b = a * b_prev + b
        a = a * a_prev
        s *= 2
    h = a * h_scr[0:1, :] + b
    h_last = h[tt - 1:tt, :]
    h_scr[...] = jnp.broadcast_to(h_last, h_scr.shape)
    o_ref[...] = (h * _gelu_tanh(ly_ref[...])).astype(o_ref.dtype)

    @pl.when(t == nt - 1)
    def _():
        hout_ref[...] = h_last


def _lru(u_all, row0, nb, t_len, conv_state, h0, cw, cb, wa, ba, wx, bx, lam, name):
    tt = _tile(t_len, 256)
    nt = t_len // tt
    rb0 = row0 // tt
    vec = lambda: pl.BlockSpec((1, L_WIDTH), lambda b, t: (0, 0))
    out, h_last = pl.pallas_call(
        functools.partial(_lru_kernel, tt=tt),
        out_shape=(jax.ShapeDtypeStruct((nb * t_len, L_WIDTH), BF16),
                   jax.ShapeDtypeStruct((nb, 1, L_WIDTH), F32)),
        grid=(nb, nt),
        in_specs=[
            pl.BlockSpec((tt, L_WIDTH), lambda b, t: (rb0 + b * nt + t, C_LX // L_WIDTH)),
            pl.BlockSpec((tt, L_WIDTH), lambda b, t: (rb0 + b * nt + t, C_LY // L_WIDTH)),
            pl.BlockSpec((None, CONV_W - 1, L_WIDTH), lambda b, t: (b, 0, 0)),
            pl.BlockSpec((None, 1, L_WIDTH), lambda b, t: (b, 0, 0)),
            pl.BlockSpec((CONV_W, L_WIDTH), lambda b, t: (0, 0)),
            vec(),
            pl.BlockSpec((L_WIDTH, L_WIDTH), lambda b, t: (0, 0)),
            vec(),
            pl.BlockSpec((L_WIDTH, L_WIDTH), lambda b, t: (0, 0)),
            vec(),
            vec(),
        ],
        out_specs=(pl.BlockSpec((tt, L_WIDTH), lambda b, t: (b * nt + t, 0)),
                   pl.BlockSpec((None, 1, L_WIDTH), lambda b, t: (b, 0, 0))),
        scratch_shapes=[pltpu.VMEM((SUBLANE + tt, L_WIDTH), F32),
                        pltpu.VMEM((SUBLANE, L_WIDTH), F32)],
        compiler_params=_params("parallel", "arbitrary"),
        name=name,
    )(u_all, u_all, conv_state, h0.reshape(nb, 1, L_WIDTH), cw, cb, wa, ba, wx, bx, lam)
    return out, h_last.reshape(nb, L_WIDTH)


def _t5_bucket(dist):
    n = jnp.maximum(dist, 0)
    max_exact = NUM_BUCKETS // 2
    nf = jnp.maximum(n, 1).astype(F32)
    large = max_exact + (jnp.log(nf / max_exact) / math.log(MAX_DISTANCE / max_exact)
                         * (NUM_BUCKETS - max_exact)).astype(jnp.int32)
    large = jnp.minimum(large, NUM_BUCKETS - 1)
    return jnp.where(n < max_exact, n, large)


def _lambda_init(layer):
    return 0.8 - 0.6 * math.exp(-0.3 * layer)


def _diff_lambda(lp, lam_init):
    s1 = jnp.sum(lp[0:1, :] * lp[1:2, :], axis=-1, keepdims=True)
    s2 = jnp.sum(lp[2:3, :] * lp[3:4, :], axis=-1, keepdims=True)
    return jnp.exp(s1) - jnp.exp(s2) + lam_init


def _softmax_step(s, m_ref, l_ref, acc_ref, pv):
    m_prev = m_ref[...]
    m_new = jnp.maximum(m_prev, jnp.max(s, axis=-1, keepdims=True))
    alpha = jnp.exp(m_prev - m_new)
    p = jnp.exp(s - m_new)
    l_ref[...] = alpha * l_ref[...] + jnp.sum(p, axis=-1, keepdims=True)
    acc_ref[...] = alpha * acc_ref[...] + pv(p)
    m_ref[...] = m_new


LOG2E = math.log2(math.e)
ATTN_COL_CHUNK = 2 * LANE


def _attn_prompt_kernel(qi_ref, ki_ref, rb_ref, q_ref, k_ref, vt_ref, lp_ref, g_ref, o_ref,
                        qs, m_scr, l_scr, acc, tab, *, tq, lam_init):
    h = pl.program_id(1)
    qi = qi_ref[pl.program_id(2)]
    ki = ki_ref[pl.program_id(2)]
    cw = min(ATTN_COL_CHUNK, tq)

    @pl.when((qi == 0) & (ki == 0))
    def _():
        kj = lax.broadcasted_iota(jnp.int32, (tq, tq), 0)
        qq = lax.broadcasted_iota(jnp.int32, (tq, tq), 1)
        far = rb_ref[h * NUM_BUCKETS + NUM_BUCKETS - 1]
        for which in range(2):
            dist = qq - kj + which * tq
            bucket = _t5_bucket(dist)
            val = jnp.zeros((tq, tq), F32)
            for b in range(NUM_BUCKETS - 2, -1, -1):
                val = jnp.where(bucket == b, (rb_ref[h * NUM_BUCKETS + b] - far) * LOG2E, val)
            if which == 0:
                val = jnp.where(dist >= 0, val, NEG)
            tab[which] = val

    @pl.when(ki == 0)
    def _():
        q = q_ref[...] * ((A_DQK ** -0.5) * LOG2E)
        lane = lax.broadcasted_iota(jnp.int32, (tq, 2 * A_DQK), 1)
        qs[0:tq, :] = jnp.where(lane < A_DQK, q, 0.0).astype(BF16)
        qs[tq:2 * tq, :] = jnp.where(lane >= A_DQK, q, 0.0).astype(BF16)
        m_scr[...] = jnp.full(m_scr.shape, NEG, F32)
        l_scr[...] = jnp.zeros(l_scr.shape, F32)
        acc[...] = jnp.zeros(acc.shape, F32)

    def step(which):
        k16 = k_ref[...].astype(BF16)
        vt16 = vt_ref[...]
        n_chunks = 2 * tq // cw
        s_next = _dot_nt(k16, qs[0:cw, :])
        for c in range(n_chunks):
            cols = slice(c * cw, (c + 1) * cw)
            s = s_next
            if c + 1 < n_chunks:
                s_next = _dot_nt(k16, qs[(c + 1) * cw:(c + 2) * cw, :])
            if which is not None:
                q0 = (c * cw) % tq
                s = s + tab[which, :, q0:q0 + cw]
            m_prev = m_scr[0:1, cols]
            m_new = jnp.maximum(m_prev, jnp.max(s, axis=0, keepdims=True))
            alpha = jnp.exp2(m_prev - m_new)
            p = jnp.exp2(s - m_new)
            l_scr[0:1, cols] = alpha * l_scr[0:1, cols] + jnp.sum(p, axis=0, keepdims=True)
            acc[:, cols] = alpha * acc[:, cols] + _dot(vt16, p.astype(BF16))
            m_scr[0:1, cols] = m_new

    @pl.when(ki < qi - 1)
    def _():
        step(None)

    @pl.when(ki == qi - 1)
    def _():
        step(1)

    @pl.when(ki == qi)
    def _():
        step(0)
        o = acc[...] / l_scr[0:1, :]
        lam = _diff_lambda(lp_ref[...], lam_init)
        a = o[:, 0:tq] - lam * o[:, tq:2 * tq]
        a = a * lax.rsqrt(jnp.mean(a * a, axis=0, keepdims=True) + EPS)
        o_ref[...] = (a.T * g_ref[...] * (1.0 - lam_init)).astype(o_ref.dtype)


def _attn_prompt(u_all, nb, t_len, rel_bias_flat, a_lambda, subln_g, layer, name):
    tq = _tile(t_len, 512, LANE)
    nq = t_len // tq
    lam_init = _lambda_init(layer)
    mp = nb * t_len
    vt = u_all[:mp, C_AV:C_AV + A_W].astype(BF16).reshape(nb, t_len, A_HEADS, A_DV)
    vt = vt.transpose(0, 2, 3, 1).reshape(nb * A_HEADS * A_DV, t_len)
    pairs = [(qi, ki) for qi in range(nq) for ki in range(qi + 1)]
    qi_of = jnp.array([p[0] for p in pairs], jnp.int32)
    ki_of = jnp.array([p[1] for p in pairs], jnp.int32)
    grid_spec = pltpu.PrefetchScalarGridSpec(
        num_scalar_prefetch=2,
        grid=(nb, A_HEADS, len(pairs)),
        in_specs=[
            pl.BlockSpec(memory_space=pltpu.SMEM),
            pl.BlockSpec((tq, A_DV), lambda b, h, p, qi, ki: (b * nq + qi[p], C_AQ // A_DV + h)),
            pl.BlockSpec((tq, A_DV), lambda b, h, p, qi, ki: (b * nq + ki[p], C_AK // A_DV + h)),
            pl.BlockSpec((A_DV, tq), lambda b, h, p, qi, ki: (b * A_HEADS + h, ki[p])),
            pl.BlockSpec((4, A_DQK), lambda b, h, p, qi, ki: (0, 0)),
            pl.BlockSpec((1, A_DV), lambda b, h, p, qi, ki: (0, 0)),
        ],
        out_specs=pl.BlockSpec((tq, A_DV), lambda b, h, p, qi, ki: (b * nq + qi[p], h)),
        scratch_shapes=[pltpu.VMEM((2 * tq, A_DV), BF16),
                        pltpu.VMEM((SUBLANE, 2 * tq), F32),
                        pltpu.VMEM((SUBLANE, 2 * tq), F32),
                        pltpu.VMEM((A_DV, 2 * tq), F32),
                        pltpu.VMEM((2, tq, tq), F32)],
    )
    return pl.pallas_call(
        functools.partial(_attn_prompt_kernel, tq=tq, lam_init=lam_init),
        out_shape=jax.ShapeDtypeStruct((mp, A_W), BF16),
        grid_spec=grid_spec,
        compiler_params=_params("arbitrary", "arbitrary", "arbitrary"),
        name=name,
    )(qi_of, ki_of, rel_bias_flat, u_all, u_all, vt, a_lambda, subln_g)


def _attn_sample_kernel(pt_ref, rb_ref, q_ref, kn_ref, vn_ref, lp_ref, g_ref, *rest,
                        n_pg, t_len, page, past, lam_init):
    kp = rest[0:n_pg]
    vp = rest[n_pg:2 * n_pg]
    o_ref = rest[2 * n_pg]
    qs, m_scr, l_scr, acc, knp, vnp = rest[2 * n_pg + 1:]
    s_id = pl.program_id(1)
    n_steps = pl.num_programs(1)
    n_cache = n_steps - 1
    rows = A_HEADS * 2 * t_len
    cols = page * A_HEADS
    row = lax.broadcasted_iota(jnp.int32, (rows, 1), 0)
    row_h = row // (2 * t_len)
    row_q = row % t_len

    def head_col(b):
        c = jnp.full((rows, 1), rb_ref[b], F32)
        for hh in range(1, A_HEADS):
            c = jnp.where(row_h == hh, rb_ref[hh * NUM_BUCKETS + b], c)
        return c

    def exact_bias(dist):
        bucket = _t5_bucket(dist)
        val = jnp.zeros(dist.shape, F32) + head_col(NUM_BUCKETS - 1)
        for b in range(NUM_BUCKETS - 2, -1, -1):
            val = jnp.where(bucket == b, head_col(b), val)
        return val

    @pl.when(s_id == 0)
    def _():
        q = q_ref[...] * (A_DQK ** -0.5)
        lane = lax.broadcasted_iota(jnp.int32, (t_len, A_DV), 1)
        for hh in range(A_HEADS):
            qh = q[:, hh * A_DV:(hh + 1) * A_DV]
            qs[hh * 2 * t_len:hh * 2 * t_len + t_len, :] = jnp.where(lane < A_DQK, qh, 0.0).astype(BF16)
            qs[hh * 2 * t_len + t_len:(hh + 1) * 2 * t_len, :] = jnp.where(lane >= A_DQK, qh, 0.0).astype(BF16)
        m_scr[...] = jnp.full(m_scr.shape, NEG, F32)
        l_scr[...] = jnp.zeros(l_scr.shape, F32)
        acc[...] = jnp.zeros(acc.shape, F32)

    def cache_step(exact):
        col = lax.broadcasted_iota(jnp.int32, (rows, cols), 1)
        own = (col % A_HEADS) == row_h
        far = head_col(NUM_BUCKETS - 1)
        parts = []
        for i in range(n_pg):
            k16 = kp[i][...].astype(BF16)
            s = _dot_nt(qs[...], k16)
            if exact:
                k_pos = (s_id * n_pg + i) * page + col // A_HEADS
                s = s + exact_bias(past + row_q - k_pos)
            else:
                s = s + far
            parts.append(jnp.where(own, s, NEG))
        s_all = jnp.concatenate(parts, axis=1) if n_pg > 1 else parts[0]

        def pv(p):
            out = None
            for i in range(n_pg):
                c = _dot(p[:, i * cols:(i + 1) * cols].astype(BF16), vp[i][...].astype(BF16))
                out = c if out is None else out + c
            return out

        _softmax_step(s_all, m_scr, l_scr, acc, pv)

    @pl.when(s_id < n_cache - 1)
    def _():
        cache_step(False)

    @pl.when(s_id == n_cache - 1)
    def _():
        cache_step(True)

    @pl.when(s_id == n_steps - 1)
    def _():
        knp[...] = jnp.zeros(knp.shape, BF16)
        vnp[...] = jnp.zeros(vnp.shape, BF16)
        kn = kn_ref[...]
        vn = vn_ref[...]
        for hh in range(A_HEADS):
            knp[hh, 0:t_len, :] = kn[:, hh * A_DV:(hh + 1) * A_DV].astype(BF16)
            vnp[hh, 0:t_len, :] = vn[:, hh * A_DV:(hh + 1) * A_DV].astype(BF16)
        kcol = lax.broadcasted_iota(jnp.int32, (rows, page), 1)
        dist = row_q - kcol
        bias = jnp.where((dist >= 0) & (kcol < t_len), exact_bias(dist), NEG)
        parts = []
        for hh in range(A_HEADS):
            parts.append(_dot_nt(qs[hh * 2 * t_len:(hh + 1) * 2 * t_len, :], knp[hh]))
        s = jnp.concatenate(parts, axis=0) + bias

        def pv(p):
            outs = []
            for hh in range(A_HEADS):
                outs.append(_dot(p[hh * 2 * t_len:(hh + 1) * 2 * t_len, :].astype(BF16), vnp[hh]))
            return jnp.concatenate(outs, axis=0)

        _softmax_step(s, m_scr, l_scr, acc, pv)
        o = acc[...] / l_scr[...]
        lam = _diff_lambda(lp_ref[...], lam_init)
        g = g_ref[...]
        for hh in range(A_HEADS):
            r0 = hh * 2 * t_len
            a = o[r0:r0 + t_len, :] - lam * o[r0 + t_len:r0 + 2 * t_len, :]
            o_ref[:, hh * A_DV:(hh + 1) * A_DV] = (_rms(a, g) * (1.0 - lam_init)).astype(o_ref.dtype)


def _attn_sample(u_all, row0, nb, t_len, cache_k, cache_v, page_table, rel_bias_flat, a_lambda,
                 subln_g, layer, name):
    depth, n_pool, page = cache_k.shape[0], cache_k.shape[1], cache_k.shape[2]
    n_pages = page_table.shape[1]
    n_pg = _tile(n_pages, 8, 1)
    n_cache = n_pages // n_pg
    past = n_pages * page
    rb0 = row0 // t_len
    rows = A_HEADS * 2 * t_len
    ck = cache_k.reshape(depth, n_pool, page * A_HEADS, A_DV)
    cv = cache_v.reshape(depth, n_pool, page * A_HEADS, A_DV)
    pt = page_table.reshape(-1)

    def page_spec(i):
        def imap(b, s, pt_ref):
            step = jnp.minimum(s, n_cache - 1)
            return (layer, pt_ref[b * n_pages + step * n_pg + i], 0, 0)
        return pl.BlockSpec((None, None, page * A_HEADS, A_DV), imap)

    grid_spec = pltpu.PrefetchScalarGridSpec(
        num_scalar_prefetch=1,
        grid=(nb, n_cache + 1),
        in_specs=[
            pl.BlockSpec(memory_space=pltpu.SMEM),
            pl.BlockSpec((t_len, A_W), lambda b, s, pt_ref: (rb0 + b, C_AQ // A_W)),
            pl.BlockSpec((t_len, A_W), lambda b, s, pt_ref: (rb0 + b, C_AK // A_W)),
            pl.BlockSpec((t_len, A_W), lambda b, s, pt_ref: (rb0 + b, C_AV // A_W)),
            pl.BlockSpec((4, A_DQK), lambda b, s, pt_ref: (0, 0)),
            pl.BlockSpec((1, A_DV), lambda b, s, pt_ref: (0, 0)),
        ] + [page_spec(i) for i in range(n_pg)] + [page_spec(i) for i in range(n_pg)],
        out_specs=pl.BlockSpec((t_len, A_W), lambda b, s, pt_ref: (b, 0)),
        scratch_shapes=[pltpu.VMEM((rows, A_DV), BF16),
                        pltpu.VMEM((rows, 1), F32),
                        pltpu.VMEM((rows, 1), F32),
                        pltpu.VMEM((rows, A_DV), F32),
                        pltpu.VMEM((A_HEADS, page, A_DV), BF16),
                        pltpu.VMEM((A_HEADS, page, A_DV), BF16)],
    )
    return pl.pallas_call(
        functools.partial(_attn_sample_kernel, n_pg=n_pg, t_len=t_len, page=page, past=past,
                          lam_init=_lambda_init(layer)),
        out_shape=jax.ShapeDtypeStruct((nb * t_len, A_W), BF16),
        grid_spec=grid_spec,
        compiler_params=_params("arbitrary", "arbitrary"),
        name=name,
    )(pt, rel_bias_flat, u_all, u_all, u_all, a_lambda, subln_g, *([ck] * n_pg), *([cv] * n_pg))


def _merge_kernel(x_ref, og_ref, oa_ref, ol_ref, gt_ref, wg_ref, wa_ref, wl_ref, wo_ref, o_ref):
    d = x_ref.shape[1]
    gt = gt_ref[...].astype(F32)
    m = gt[:, 0:d] * _dot(og_ref[...], wg_ref[...])
    m = m + gt[:, d:2 * d] * _dot(oa_ref[...], wa_ref[...])
    m = m + gt[:, 2 * d:3 * d] * _dot(ol_ref[...], wl_ref[...])
    o_ref[...] = x_ref[...] + _dot(m.astype(BF16), wo_ref[...])


def _merge(x, o_g, o_a, o_l, gates, wg, wa, wl, wo, name):
    m, d = x.shape
    tm = _tile(m, 640)
    row = lambda w: pl.BlockSpec((tm, w), lambda i: (i, 0))
    full = lambda a: pl.BlockSpec(a.shape, lambda i: (0, 0))
    return pl.pallas_call(
        _merge_kernel,
        out_shape=jax.ShapeDtypeStruct((m, d), F32),
        grid=(m // tm,),
        in_specs=[row(d), row(G_W), row(A_W), row(L_WIDTH), row(3 * d),
                  full(wg), full(wa), full(wl), full(wo)],
        out_specs=row(d),
        compiler_params=_params("parallel"),
        name=name,
    )(x, o_g, o_a, o_l, gates, wg, wa, wl, wo)


def _ffn_kernel(x_ref, g_ref, w1_ref, w3_ref, w2_ref, o_ref, h_scr, acc):
    f = pl.program_id(1)

    @pl.when(f == 0)
    def _():
        x = x_ref[...]
        h_scr[...] = _rms(x, g_ref[...]).astype(BF16)
        acc[...] = x

    h = h_scr[...]
    t = _silu(_dot(h, w1_ref[...])) * _dot(h, w3_ref[...])
    acc[...] += _dot(t.astype(BF16), w2_ref[...])

    @pl.when(f == pl.num_programs(1) - 1)
    def _():
        o_ref[...] = acc[...]


def _ffn(x, g, w1, w3, w2, name):
    m, d = x.shape
    ff = w1.shape[1]
    tm = _tile(m, 1280)
    tf = _tile(ff, 512, LANE)
    return pl.pallas_call(
        _ffn_kernel,
        out_shape=jax.ShapeDtypeStruct((m, d), F32),
        grid=(m // tm, ff // tf),
        in_specs=[
            pl.BlockSpec((tm, d), lambda i, f: (i, 0)),
            pl.BlockSpec((1, d), lambda i, f: (0, 0)),
            pl.BlockSpec((d, tf), lambda i, f: (0, f)),
            pl.BlockSpec((d, tf), lambda i, f: (0, f)),
            pl.BlockSpec((tf, d), lambda i, f: (f, 0)),
        ],
        out_specs=pl.BlockSpec((tm, d), lambda i, f: (i, 0)),
        scratch_shapes=[pltpu.VMEM((tm, d), BF16), pltpu.VMEM((tm, d), F32)],
        compiler_params=_params("parallel", "arbitrary"),
        name=name,
    )(x, g, w1, w3, w2)


def _router_kernel(x_ref, g_ref, wr_ref, o_ref):
    h = _rms(x_ref[...], g_ref[...])
    wr_hi, wr_lo = _split(wr_ref[...])
    logits = _dot_split(h, wr_hi, wr_lo)
    lane = lax.broadcasted_iota(jnp.int32, logits.shape, 1)
    logits = jnp.where(lane < N_EXPERTS, logits, -jnp.inf)
    m1 = jnp.max(logits, axis=-1, keepdims=True)
    i1 = jnp.min(jnp.where(logits == m1, lane, LANE), axis=-1, keepdims=True)
    rest = jnp.where(lane == i1, -jnp.inf, logits)
    m2 = jnp.max(rest, axis=-1, keepdims=True)
    i2 = jnp.min(jnp.where(rest == m2, lane, LANE), axis=-1, keepdims=True)
    e2 = jnp.exp(m2 - m1)
    den = 1.0 + e2
    o_ref[...] = jnp.where(lane == i1, 1.0 / den, 0.0) + jnp.where(lane == i2, e2 / den, 0.0)


def _router(x, g, wr, name):
    m, d = x.shape
    tm = _tile(m, 1280)
    return pl.pallas_call(
        _router_kernel,
        out_shape=jax.ShapeDtypeStruct((m, LANE), F32),
        grid=(m // tm,),
        in_specs=[pl.BlockSpec((tm, d), lambda i: (i, 0)),
                  pl.BlockSpec((1, d), lambda i: (0, 0)),
                  pl.BlockSpec((d, LANE), lambda i: (0, 0))],
        out_specs=pl.BlockSpec((tm, LANE), lambda i: (i, 0)),
        compiler_params=_params("parallel"),
        name=name,
    )(x, g, wr)


def _moe_kernel(x_ref, g_ref, gate_ref, w1_ref, w3_ref, w2_ref, fg_ref, o_ref, h_scr, acc):
    e = pl.program_id(1)
    f = pl.program_id(2)
    first = (e == 0) & (f == 0)
    last = (e == pl.num_programs(1) - 1) & (f == pl.num_programs(2) - 1)

    @pl.when(first)
    def _():
        x = x_ref[...]
        h_scr[...] = _rms(x, g_ref[...]).astype(BF16)
        acc[...] = x

    gate = gate_ref[...]
    lane = lax.broadcasted_iota(jnp.int32, gate.shape, 1)
    ge = jnp.sum(jnp.where(lane == e, gate, 0.0), axis=-1, keepdims=True)
    h = h_scr[...]
    t = _silu(_dot(h, w1_ref[...])) * _dot(h, w3_ref[...])
    acc[...] += ge * _dot(t.astype(BF16), w2_ref[...])

    @pl.when(last)
    def _():
        o_ref[...] = _rms(acc[...], fg_ref[...])


def _moe(x, g, gate, w1, w3, w2, final_g, name):
    m, d = x.shape
    ne, _, ff = w1.shape
    tm = _tile(m, 1280)
    tf = _tile(ff, 512, LANE)
    return pl.pallas_call(
        _moe_kernel,
        out_shape=jax.ShapeDtypeStruct((m, d), F32),
        grid=(m // tm, ne, ff // tf),
        in_specs=[
            pl.BlockSpec((tm, d), lambda i, e, f: (i, 0)),
            pl.BlockSpec((1, d), lambda i, e, f: (0, 0)),
            pl.BlockSpec((tm, LANE), lambda i, e, f: (i, 0)),
            pl.BlockSpec((None, d, tf), lambda i, e, f: (e, 0, f)),
            pl.BlockSpec((None, d, tf), lambda i, e, f: (e, 0, f)),
            pl.BlockSpec((None, tf, d), lambda i, e, f: (e, f, 0)),
            pl.BlockSpec((1, d), lambda i, e, f: (0, 0)),
        ],
        out_specs=pl.BlockSpec((tm, d), lambda i, e, f: (i, 0)),
        scratch_shapes=[pltpu.VMEM((tm, d), BF16), pltpu.VMEM((tm, d), F32)],
        compiler_params=_params("parallel", "arbitrary", "arbitrary"),
        name=name,
    )(x, g, gate, w1, w3, w2, final_g)


def _final_norm_kernel(x_ref, g_ref, o_ref):
    o_ref[...] = _rms(x_ref[...], g_ref[...])


def _final_norm(x, g, name):
    m, d = x.shape
    tm = _tile(m, 1280)
    return pl.pallas_call(
        _final_norm_kernel,
        out_shape=jax.ShapeDtypeStruct((m, d), F32),
        grid=(m // tm,),
        in_specs=[pl.BlockSpec((tm, d), lambda i: (i, 0)), pl.BlockSpec((1, d), lambda i: (0, 0))],
        out_specs=pl.BlockSpec((tm, d), lambda i: (i, 0)),
        compiler_params=_params("parallel"),
        name=name,
    )(x, g)


def _split_w_in(w):
    o_gb = 4 * G_W
    o_aq = o_gb + 2 * G_HEADS
    pad = jnp.zeros((w.shape[0], LANE - 2 * G_HEADS), w.dtype)
    return (jnp.concatenate([w[:, :o_gb], w[:, o_aq:]], axis=1),
            jnp.concatenate([w[:, o_gb:o_aq], pad], axis=1))


def _block_diag(w):
    n, d, _ = w.shape
    eye = jnp.eye(n, dtype=w.dtype)
    return (eye[:, None, :, None] * w[:, :, None, :]).reshape(n * d, n * d)


def kernel(x_prompt, x_sample, cache_k, cache_v, page_table, state_gdn, state_gdn_conv, state_lru, state_lru_conv, norm1_g, norm2_g, final_g, w_in, w_gate, b_gate, g_conv_w, g_a_log, g_dt_bias, g_norm_g, a_lambda, a_subln_g, rel_bias, l_conv_w, l_conv_b, l_wa, l_ba, l_wx, l_bx, l_lambda, w_br_g, w_br_a, w_br_l, w_out, f_w1, f_w3, f_w2, m_router, m_w1, m_w3, m_w2):
    bp, tp, d = x_prompt.shape
    bs, ts, _ = x_sample.shape
    depth = w_in.shape[0]
    mp = bp * tp
    ms = bs * ts
    x = jnp.concatenate([x_prompt.reshape(mp, d), x_sample.reshape(ms, d)], axis=0)
    rb_flat = rel_bias.T.reshape(-1)
    row = lambda v: v.reshape(1, -1)

    k_p, v_p, k_s, v_s = [], [], [], []
    sg_p, sg_s, cg_p, cg_s, sl_p, sl_s, cl_p, cl_s = [], [], [], [], [], [], [], []
    y = None
    for l in range(depth):
        w_main, w_gba = _split_w_in(w_in[l])
        u = _norm_matmul(x, row(norm1_g[l]), w_main.astype(BF16), jnp.zeros((1, U_COLS), F32),
                         None, F32, f"in_proj_{l}")
        gba = _norm_matmul(x, row(norm1_g[l]), w_gba.astype(BF16), jnp.zeros((1, LANE), F32),
                           None, F32, f"in_proj_gba_{l}")
        gates = _norm_matmul(x, row(norm1_g[l]), w_gate[l].astype(BF16), row(b_gate[l]),
                             "sigmoid", BF16, f"gate_proj_{l}")

        ab = jnp.zeros((2, LANE), F32)
        ab = ab.at[0, G_HEADS:2 * G_HEADS].set(g_a_log[l]).at[1, G_HEADS:2 * G_HEADS].set(g_dt_bias[l])
        ng = row(g_norm_g[l])
        og_p, s_p = _gdn(u, gba, 0, bp, tp, jnp.zeros((bp, CONV_W - 1, 3 * G_W), F32),
                         jnp.zeros((bp, G_HEADS, G_DK, G_DV), F32), g_conv_w[l], ab, ng, f"gdn_prompt_{l}")
        og_s, s_s = _gdn(u, gba, mp, bs, ts, state_gdn_conv[l], state_gdn[l], g_conv_w[l], ab, ng,
                         f"gdn_sample_{l}")

        sub_g = row(a_subln_g[l])
        oa_p = _attn_prompt(u, bp, tp, rb_flat, a_lambda[l], sub_g, l, f"attn_prompt_{l}")
        oa_s = _attn_sample(u, mp, bs, ts, cache_k, cache_v, page_table, rb_flat, a_lambda[l], sub_g, l,
                            f"attn_sample_{l}")

        wa_bd = _block_diag(l_wa[l]).astype(BF16)
        wx_bd = _block_diag(l_wx[l]).astype(BF16)
        lru_args = (l_conv_w[l], row(l_conv_b[l]), wa_bd, row(l_ba[l]), wx_bd, row(l_bx[l]), row(l_lambda[l]))
        ol_p, h_p = _lru(u, 0, bp, tp, jnp.zeros((bp, CONV_W - 1, L_WIDTH), F32),
                         jnp.zeros((bp, L_WIDTH), F32), *lru_args, f"lru_prompt_{l}")
        ol_s, h_s = _lru(u, mp, bs, ts, state_lru_conv[l], state_lru[l], *lru_args, f"lru_sample_{l}")

        x = _merge(x, jnp.concatenate([og_p, og_s]), jnp.concatenate([oa_p, oa_s]),
                   jnp.concatenate([ol_p, ol_s]), gates, w_br_g[l].astype(BF16), w_br_a[l].astype(BF16),
                   w_br_l[l].astype(BF16), w_out[l].astype(BF16), f"merge_{l}")

        j = l // 2
        if l % 2 == 0:
            x = _ffn(x, row(norm2_g[l]), f_w1[j].astype(BF16), f_w3[j].astype(BF16), f_w2[j].astype(BF16),
                     f"ffn_{l}")
            if l == depth - 1:
                y = _final_norm(x, row(final_g), "final_norm")
        else:
            wr = jnp.zeros((d, LANE), F32).at[:, :N_EXPERTS].set(m_router[j])
            gate = _router(x, row(norm2_g[l]), wr, f"router_{l}")
            fg = row(final_g) if l == depth - 1 else None
            if fg is None:
                raise NotImplementedError("mixture layer is only fused with the final norm")
            y = _moe(x, row(norm2_g[l]), gate, m_w1[j].astype(BF16), m_w3[j].astype(BF16),
                     m_w2[j].astype(BF16), fg, f"moe_{l}")

        def tail_rows(c0, width):
            keep = CONV_W - 1
            p = jnp.stack([u[(b + 1) * tp - keep:(b + 1) * tp, c0:c0 + width] for b in range(bp)])
            s = u[mp:, c0:c0 + width].reshape(bs, ts, width)[:, ts - keep:]
            return p, s

        k_p.append(u[:mp, C_AK:C_AK + A_W].reshape(bp, tp, A_HEADS, A_DV))
        v_p.append(u[:mp, C_AV:C_AV + A_W].reshape(bp, tp, A_HEADS, A_DV))
        k_s.append(u[mp:, C_AK:C_AK + A_W].reshape(bs, ts, A_HEADS, A_DV))
        v_s.append(u[mp:, C_AV:C_AV + A_W].reshape(bs, ts, A_HEADS, A_DV))
        sg_p.append(s_p)
        sg_s.append(s_s)
        cp, cs = tail_rows(C_GQKV, 3 * G_W)
        cg_p.append(cp)
        cg_s.append(cs)
        sl_p.append(h_p)
        sl_s.append(h_s)
        cp, cs = tail_rows(C_LX, L_WIDTH)
        cl_p.append(cp)
        cl_s.append(cs)

    st = jnp.stack
    return (y[:mp].reshape(bp, tp, d), y[mp:].reshape(bs, ts, d),
            st(k_p), st(v_p), st(k_s), st(v_s), st(sg_p), st(sg_s), st(cg_p), st(cg_s),
            st(sl_p), st(sl_s), st(cl_p), st(cl_s))
```

```python
import functools
import math

import jax
import jax.numpy as jnp
from jax import lax
from jax.experimental import pallas as pl
from jax.experimental.pallas import tpu as pltpu

F32 = jnp.float32
BF16 = jnp.bfloat16
EPS = 1e-6
NEG = -1e30

G_HEADS, G_DK, G_DV = 4, 128, 128
G_W = G_HEADS * G_DK
GDN_CHUNK = 64
CONV_W = 4
A_HEADS, A_DQK, A_DV = 4, 64, 128
A_W = A_HEADS * A_DV
L_WIDTH, L_BLOCKS = 512, 8
LRU_C = 8.0
NUM_BUCKETS, MAX_DISTANCE = 32, 128
N_EXPERTS, TOP_K = 8, 2
LANE = 128
SUBLANE = 8
VMEM_LIMIT = 56 * 1024 * 1024

C_GQKV = 0
C_GZ = 3 * G_W
C_AQ = 4 * G_W
C_AK = C_AQ + A_W
C_AV = C_AK + A_W
C_LX = C_AV + A_W
C_LY = C_LX + L_WIDTH
U_COLS = C_LY + L_WIDTH


def _tile(n, target, mult=SUBLANE):
    best = None
    for d in range(mult, min(n, target) + 1, mult):
        if n % d == 0:
            best = d
    return best if best is not None else n


def _params(*sem):
    return pltpu.CompilerParams(dimension_semantics=sem, vmem_limit_bytes=VMEM_LIMIT)


def _dot(a, b):
    return jnp.dot(a, b, preferred_element_type=F32)


def _dot_nt(a, b):
    return lax.dot_general(a, b, (((1,), (1,)), ((), ())), preferred_element_type=F32)


def _dot_tn(a, b):
    return lax.dot_general(a, b, (((0,), (0,)), ((), ())), preferred_element_type=F32)


def _split(x):
    hi = x.astype(BF16)
    return hi, (x - hi.astype(F32)).astype(BF16)


def _dot_split(a, b_hi, b_lo):
    a_hi, a_lo = _split(a)
    return _dot(a_hi, b_hi) + (_dot(a_hi, b_lo) + _dot(a_lo, b_hi))


def _dot_lhs_exact(a16, b):
    b_hi, b_lo = _split(b)
    return _dot(a16, b_hi) + _dot(a16, b_lo)


def _sigmoid(x):
    return 1.0 / (1.0 + jnp.exp(-x))


def _silu(x):
    return x * _sigmoid(x)


def _softplus(x):
    return jnp.maximum(x, 0.0) + jnp.log1p(jnp.exp(-jnp.abs(x)))


def _rms(x, g):
    return x * lax.rsqrt(jnp.mean(x * x, axis=-1, keepdims=True) + EPS) * g


def _norm_mm_kernel(x_ref, g_ref, w_ref, b_ref, o_ref, h_scr, *, act):
    @pl.when(pl.program_id(1) == 0)
    def _():
        h_scr[...] = _rms(x_ref[...], g_ref[...]).astype(BF16)

    acc = _dot(h_scr[...], w_ref[...]) + b_ref[...]
    if act == "sigmoid":
        acc = _sigmoid(acc)
    o_ref[...] = acc.astype(o_ref.dtype)


def _norm_matmul(x, g, w, b, act, out_dtype, name):
    m, k = x.shape
    n = w.shape[1]
    tm = _tile(m, 1280)
    tn = _tile(n, 512, LANE)
    return pl.pallas_call(
        functools.partial(_norm_mm_kernel, act=act),
        out_shape=jax.ShapeDtypeStruct((m, n), out_dtype),
        grid=(m // tm, n // tn),
        in_specs=[
            pl.BlockSpec((tm, k), lambda i, j: (i, 0)),
            pl.BlockSpec((1, k), lambda i, j: (0, 0)),
            pl.BlockSpec((k, tn), lambda i, j: (0, j)),
            pl.BlockSpec((1, tn), lambda i, j: (0, j)),
        ],
        out_specs=pl.BlockSpec((tm, tn), lambda i, j: (i, j)),
        scratch_shapes=[pltpu.VMEM((tm, k), BF16)],
        compiler_params=_params("parallel", "arbitrary"),
        name=name,
    )(x, g, w, b)


def _gdn_kernel(qkv_ref, z_ref, gba_ref, cs_ref, s0_ref, cw_ref, ab_ref, ng_ref,
                o_ref, sout_ref, xe, ys, bg, s_scr, *, tt, chunk, rows):
    t = pl.program_id(1)
    nt = pl.num_programs(1)
    hc = G_HEADS * chunk

    @pl.when(t == 0)
    def _():
        for h in range(G_HEADS):
            s_scr[:, h * G_DV:(h + 1) * G_DV] = s0_ref[h]
        xe[...] = jnp.zeros(xe.shape, F32)
        xe[SUBLANE - (CONV_W - 1):SUBLANE, :] = cs_ref[...]

    xe[SUBLANE:SUBLANE + tt, :] = qkv_ref[...]
    cw = cw_ref[...]
    base = SUBLANE - (CONV_W - 1)
    y = xe[base:base + rows, :] * cw[0:1, :]
    for j in range(1, CONV_W):
        y = y + xe[base + j:base + j + rows, :] * cw[j:j + 1, :]
    xe[0:SUBLANE, :] = xe[tt:tt + SUBLANE, :]

    gba = gba_ref[...]
    if rows > tt:
        gba = jnp.concatenate([gba, jnp.zeros((rows - tt, LANE), F32)], axis=0)
    ab = ab_ref[...]
    lane_g = lax.broadcasted_iota(jnp.int32, (rows, LANE), 1)
    bgc = jnp.where(lane_g < G_HEADS, _sigmoid(gba), -jnp.exp(ab[0:1, :]) * _softplus(gba + ab[1:2, :]))
    y = _silu(y)
    if rows > tt:
        live = lax.broadcasted_iota(jnp.int32, (rows, 1), 0) < tt
        y = jnp.where(live, y, 0.0)
        bgc = jnp.where(live, bgc, 0.0)
    bg[...] = bgc
    for h in range(G_HEADS):
        q = y[:, h * G_DK:(h + 1) * G_DK]
        k = y[:, G_W + h * G_DK:G_W + (h + 1) * G_DK]
        ys[:, h * G_DK:(h + 1) * G_DK] = q * lax.rsqrt(jnp.sum(q * q, axis=-1, keepdims=True) + 1e-6) * (G_DK ** -0.5)
        ys[:, G_W + h * G_DK:G_W + (h + 1) * G_DK] = k * lax.rsqrt(jnp.sum(k * k, axis=-1, keepdims=True) + 1e-6)
    ys[:, 2 * G_W:3 * G_W] = y[:, 2 * G_W:3 * G_W]

    def iota(shape, d):
        return lax.broadcasted_iota(jnp.int32, shape, d)

    r_c = iota((chunk, hc), 0)
    j_c = iota((chunk, hc), 1) % chunk
    tril_cat = r_c >= j_c
    strict_cat = r_c > j_c
    eye_cat = (r_c == j_c).astype(F32)
    triu_cat = (r_c <= j_c).astype(F32)
    tril_f = jnp.where(iota((chunk, chunk), 0) >= iota((chunk, chunk), 1), 1.0, 0.0).astype(BF16)
    ones_cc = jnp.ones((chunk, chunk), BF16)
    l_s = iota((LANE, hc + 2 * G_W), 0)
    n_s = iota((LANE, hc + 2 * G_W), 1)
    src_lane = jnp.where(n_s < hc, G_HEADS + n_s // chunk,
                         jnp.where(n_s < hc + G_W, G_HEADS + (n_s - hc) // G_DK,
                                   (n_s - hc - G_W) // G_DK))
    sel = jnp.where(l_s == src_lane, 1.0, 0.0).astype(BF16)
    bd_cc = iota((hc, hc), 0) // chunk == iota((hc, hc), 1) // chunk
    bd_cd = iota((hc, G_W), 0) // chunk == iota((hc, G_W), 1) // G_DK
    bd_dd = iota((G_W, G_W), 0) // G_DK == iota((G_W, G_W), 1) // G_DK
    ng = ng_ref[...]
    n_doubling = int(math.log2(chunk)) - 1

    def blockdiag(x, mask):
        reps = mask.shape[0] // x.shape[0]
        return jnp.where(mask, jnp.concatenate([x] * reps, axis=0), 0.0)

    def phase_a(c0):
        bg_hi, bg_lo = _split(bg[c0:c0 + chunk, :])
        selm = _dot(bg_hi, sel) + _dot(bg_lo, sel)
        cum = _dot_lhs_exact(tril_f, selm[:, 0:hc + G_W])
        gi_cat = cum[:, 0:hc]
        gi = cum[:, hc:hc + G_W]
        gj_cat = _dot_lhs_exact(ones_cc, selm[:, 0:hc] * triu_cat)
        beta = selm[:, hc + G_W:hc + 2 * G_W]
        decay = jnp.where(tril_cat, jnp.exp(gi_cat - gj_cat), 0.0)
        egi = jnp.exp(gi)
        glast = gi[chunk - 1:chunk, :]
        q = ys[c0:c0 + chunk, 0:G_W]
        k = ys[c0:c0 + chunk, G_W:2 * G_W]
        v = ys[c0:c0 + chunk, 2 * G_W:3 * G_W]
        kb = k * beta
        bdk = blockdiag(k, bd_cd).astype(BF16)
        la = _dot_nt(jnp.concatenate([kb, q], axis=0).astype(BF16), bdk)
        lmat = jnp.where(strict_cat, la[0:chunk] * decay, 0.0)
        a_in = la[chunk:2 * chunk] * decay
        pw = -lmat
        tinv = eye_cat + pw
        bd_hi, bd_lo = _split(blockdiag(pw, bd_cc))
        for _ in range(n_doubling):
            pw = _dot_split(pw, bd_hi, bd_lo)
            bd_hi, bd_lo = _split(blockdiag(pw, bd_cc))
            tinv = tinv + _dot_split(tinv, bd_hi, bd_lo)
        rhs = jnp.concatenate([blockdiag(v * beta, bd_cd), blockdiag(kb * egi, bd_cd)], axis=1)
        uw = _dot(tinv.astype(BF16), rhs.astype(BF16))
        return dict(u=uw[:, 0:G_W], w=uw[:, G_W:2 * G_W], qe=q * egi, a=a_in.astype(BF16),
                    kd=(k * jnp.exp(glast - gi)).astype(BF16), gl=jnp.exp(glast))

    def phase_b(c0, a):
        s_cat = s_scr[...]
        bds = blockdiag(s_cat, bd_dd).astype(BF16)
        ws = _dot(jnp.concatenate([a["w"], a["qe"]], axis=0).astype(BF16), bds)
        v_new = a["u"] - ws[0:chunk]
        o = ws[chunk:2 * chunk] + _dot(a["a"], blockdiag(v_new, bd_cd).astype(BF16))
        v16 = v_new.astype(BF16)
        n_out = min(tt, chunk)
        for h in range(G_HEADS):
            hs = slice(h * G_DV, (h + 1) * G_DV)
            s_scr[:, hs] = s_cat[:, hs] * a["gl"][:, hs] + _dot_tn(a["kd"][:, hs], v16[:, hs])
            zz = z_ref[c0:c0 + n_out, hs]
            o_ref[c0:c0 + n_out, hs] = (_rms(o[0:n_out, hs], ng) * _silu(zz)).astype(o_ref.dtype)

    n_chunks = rows // chunk
    nxt = phase_a(0)
    for c in range(n_chunks):
        cur = nxt
        if c + 1 < n_chunks:
            nxt = phase_a((c + 1) * chunk)
        phase_b(c * chunk, cur)

    @pl.when(t == nt - 1)
    def _():
        for h in range(G_HEADS):
            sout_ref[h] = s_scr[:, h * G_DV:(h + 1) * G_DV]


def _gdn(u_all, gba_all, row0, nb, t_len, conv_state, s0, conv_w, ab, norm_g, name):
    chunk = GDN_CHUNK
    if t_len >= chunk:
        tt = _tile(t_len, 256, chunk)
        rows = tt
    else:
        tt = t_len
        rows = chunk
    nt = t_len // tt
    rb0 = row0 // tt
    w3 = 3 * G_W
    return pl.pallas_call(
        functools.partial(_gdn_kernel, tt=tt, chunk=chunk, rows=rows),
        out_shape=(jax.ShapeDtypeStruct((nb * t_len, G_W), BF16),
                   jax.ShapeDtypeStruct((nb, G_HEADS, G_DK, G_DV), F32)),
        grid=(nb, nt),
        in_specs=[
            pl.BlockSpec((tt, w3), lambda b, t: (rb0 + b * nt + t, C_GQKV // w3)),
            pl.BlockSpec((tt, G_W), lambda b, t: (rb0 + b * nt + t, C_GZ // G_W)),
            pl.BlockSpec((tt, LANE), lambda b, t: (rb0 + b * nt + t, 0)),
            pl.BlockSpec((None, CONV_W - 1, w3), lambda b, t: (b, 0, 0)),
            pl.BlockSpec((None, G_HEADS, G_DK, G_DV), lambda b, t: (b, 0, 0, 0)),
            pl.BlockSpec((CONV_W, w3), lambda b, t: (0, 0)),
            pl.BlockSpec((2, LANE), lambda b, t: (0, 0)),
            pl.BlockSpec((1, G_DV), lambda b, t: (0, 0)),
        ],
        out_specs=(pl.BlockSpec((tt, G_W), lambda b, t: (b * nt + t, 0)),
                   pl.BlockSpec((None, G_HEADS, G_DK, G_DV), lambda b, t: (b, 0, 0, 0))),
        scratch_shapes=[pltpu.VMEM((SUBLANE + rows, w3), F32),
                        pltpu.VMEM((rows, w3), F32),
                        pltpu.VMEM((rows, LANE), F32),
                        pltpu.VMEM((G_DK, G_HEADS * G_DV), F32)],
        compiler_params=_params("parallel", "arbitrary"),
        name=name,
    )(u_all, u_all, gba_all, conv_state, s0, conv_w, ab, norm_g)


def _gelu_tanh(x):
    return 0.5 * x * (1.0 + jnp.tanh(math.sqrt(2.0 / math.pi) * (x + 0.044715 * (x * x * x))))


def _lru_kernel(lx_ref, ly_ref, cs_ref, h0_ref, cw_ref, cb_ref, wa_ref, ba_ref, wx_ref, bx_ref,
                lam_ref, o_ref, hout_ref, xe, h_scr, *, tt):
    t = pl.program_id(1)
    nt = pl.num_programs(1)

    @pl.when(t == 0)
    def _():
        h_scr[...] = jnp.broadcast_to(h0_ref[...], h_scr.shape)
        xe[0:SUBLANE, :] = jnp.zeros((SUBLANE, L_WIDTH), F32)
        xe[SUBLANE - (CONV_W - 1):SUBLANE, :] = cs_ref[...]

    xe[SUBLANE:SUBLANE + tt, :] = lx_ref[...]
    cw = cw_ref[...]
    base = SUBLANE - (CONV_W - 1)
    xc = xe[base:base + tt, :] * cw[0:1, :]
    for j in range(1, CONV_W):
        xc = xc + xe[base + j:base + j + tt, :] * cw[j:j + 1, :]
    xe[0:SUBLANE, :] = xe[tt:tt + SUBLANE, :]
    xc = xc + cb_ref[...]
    x16 = xc.astype(BF16)
    r = _sigmoid(_dot(x16, wa_ref[...]) + ba_ref[...])
    i = _sigmoid(_dot(x16, wx_ref[...]) + bx_ref[...])
    log_a = -LRU_C * r * _softplus(-lam_ref[...])
    a = jnp.exp(log_a)
    b = jnp.sqrt(jnp.tanh(-log_a) * (a * a + 1.0)) * (i * xc)
    row = lax.broadcasted_iota(jnp.int32, (tt, 1), 0)
    s = 1
    while s < tt:
        keep = row >= s
        a_prev = jnp.where(keep, pltpu.roll(a, s, 0), 1.0)
        b_prev = jnp.where(keep, pltpu.roll(b, s, 0), 0.0)
        b = a * b_prev + b
        a = a * a_prev
        s *= 2
    h = a * h_scr[0:1, :] + b
    h_last = h[tt - 1:tt, :]
    h_scr[...] = jnp.broadcast_to(h_last, h_scr.shape)
    o_ref[...] = (h * _gelu_tanh(ly_ref[...])).astype(o_ref.dtype)

    @pl.when(t == nt - 1)
    def _():
        hout_ref[...] = h_last


def _lru(u_all, row0, nb, t_len, conv_state, h0, cw, cb, wa, ba, wx, bx, lam, name):
    tt = _tile(t_len, 256)
    nt = t_len // tt
    rb0 = row0 // tt
    vec = lambda: pl.BlockSpec((1, L_WIDTH), lambda b, t: (0, 0))
    out, h_last = pl.pallas_call(
        functools.partial(_lru_kernel, tt=tt),
        out_shape=(jax.ShapeDtypeStruct((nb * t_len, L_WIDTH), BF16),
                   jax.ShapeDtypeStruct((nb, 1, L_WIDTH), F32)),
        grid=(nb, nt),
        in_specs=[
            pl.BlockSpec((tt, L_WIDTH), lambda b, t: (rb0 + b * nt + t, C_LX // L_WIDTH)),
            pl.BlockSpec((tt, L_WIDTH), lambda b, t: (rb0 + b * nt + t, C_LY // L_WIDTH)),
            pl.BlockSpec((None, CONV_W - 1, L_WIDTH), lambda b, t: (b, 0, 0)),
            pl.BlockSpec((None, 1, L_WIDTH), lambda b, t: (b, 0, 0)),
            pl.BlockSpec((CONV_W, L_WIDTH), lambda b, t: (0, 0)),
            vec(),
            pl.BlockSpec((L_WIDTH, L_WIDTH), lambda b, t: (0, 0)),
            vec(),
            pl.BlockSpec((L_WIDTH, L_WIDTH), lambda b, t: (0, 0)),
            vec(),
            vec(),
        ],
        out_specs=(pl.BlockSpec((tt, L_WIDTH), lambda b, t: (b * nt + t, 0)),
                   pl.BlockSpec((None, 1, L_WIDTH), lambda b, t: (b, 0, 0))),
        scratch_shapes=[pltpu.VMEM((SUBLANE + tt, L_WIDTH), F32),
                        pltpu.VMEM((SUBLANE, L_WIDTH), F32)],
        compiler_params=_params("parallel", "arbitrary"),
        name=name,
    )(u_all, u_all, conv_state, h0.reshape(nb, 1, L_WIDTH), cw, cb, wa, ba, wx, bx, lam)
    return out, h_last.reshape(nb, L_WIDTH)


def _t5_bucket(dist):
    n = jnp.maximum(dist, 0)
    max_exact = NUM_BUCKETS // 2
    nf = jnp.maximum(n, 1).astype(F32)
    large = max_exact + (jnp.log(nf / max_exact) / math.log(MAX_DISTANCE / max_exact)
                         * (NUM_BUCKETS - max_exact)).astype(jnp.int32)
    large = jnp.minimum(large, NUM_BUCKETS - 1)
    return jnp.where(n < max_exact, n, large)


def _lambda_init(layer):
    return 0.8 - 0.6 * math.exp(-0.3 * layer)


def _diff_lambda(lp, lam_init):
    s1 = jnp.sum(lp[0:1, :] * lp[1:2, :], axis=-1, keepdims=True)
    s2 = jnp.sum(lp[2:3, :] * lp[3:4, :], axis=-1, keepdims=True)
    return jnp.exp(s1) - jnp.exp(s2) + lam_init


def _softmax_step(s, m_ref, l_ref, acc_ref, pv):
    m_prev = m_ref[...]
    m_new = jnp.maximum(m_prev, jnp.max(s, axis=-1, keepdims=True))
    alpha = jnp.exp2(m_prev - m_new)
    p = jnp.exp2(s - m_new)
    l_ref[...] = alpha * l_ref[...] + jnp.sum(p, axis=-1, keepdims=True)
    acc_ref[...] = alpha * acc_ref[...] + pv(p)
    m_ref[...] = m_new


LOG2E = math.log2(math.e)
ATTN_COL_CHUNK = LANE


def _attn_prompt_kernel(qi_ref, ki_ref, rb_ref, q_ref, k_ref, vt_ref, lp_ref, g_ref, o_ref,
                        qs, m_scr, l_scr, acc, tab, *, tq, lam_init):
    h = pl.program_id(1)
    qi = qi_ref[pl.program_id(2)]
    ki = ki_ref[pl.program_id(2)]
    cw = min(ATTN_COL_CHUNK, tq)

    @pl.when((qi == 0) & (ki == 0))
    def _():
        kj = lax.broadcasted_iota(jnp.int32, (tq, tq), 0)
        qq = lax.broadcasted_iota(jnp.int32, (tq, tq), 1)
        far = rb_ref[h * NUM_BUCKETS + NUM_BUCKETS - 1]
        for which in range(2):
            dist = qq - kj + which * tq
            bucket = _t5_bucket(dist)
            val = jnp.zeros((tq, tq), F32)
            for b in range(NUM_BUCKETS - 2, -1, -1):
                val = jnp.where(bucket == b, (rb_ref[h * NUM_BUCKETS + b] - far) * LOG2E, val)
            if which == 0:
                val = jnp.where(dist >= 0, val, NEG)
            tab[which] = val

    @pl.when(ki == 0)
    def _():
        q = q_ref[...] * ((A_DQK ** -0.5) * LOG2E)
        lane = lax.broadcasted_iota(jnp.int32, (tq, 2 * A_DQK), 1)
        qs[0:tq, :] = jnp.where(lane < A_DQK, q, 0.0).astype(BF16)
        qs[tq:2 * tq, :] = jnp.where(lane >= A_DQK, q, 0.0).astype(BF16)
        m_scr[...] = jnp.full(m_scr.shape, NEG, F32)
        l_scr[...] = jnp.zeros(l_scr.shape, F32)
        acc[...] = jnp.zeros(acc.shape, F32)

    def step(which):
        k16 = k_ref[...].astype(BF16)
        vt16 = vt_ref[...]
        n_chunks = 2 * tq // cw
        s_next = _dot_nt(k16, qs[0:cw, :])
        for c in range(n_chunks):
            cols = slice(c * cw, (c + 1) * cw)
            s = s_next
            if c + 1 < n_chunks:
                s_next = _dot_nt(k16, qs[(c + 1) * cw:(c + 2) * cw, :])
            if which is not None:
                q0 = (c * cw) % tq
                s = s + tab[which, :, q0:q0 + cw]
            m_prev = m_scr[0:1, cols]
            m_new = jnp.maximum(m_prev, jnp.max(s, axis=0, keepdims=True))
            alpha = jnp.exp2(m_prev - m_new)
            p = jnp.exp2(s - m_new)
            l_scr[0:1, cols] = alpha * l_scr[0:1, cols] + jnp.sum(p, axis=0, keepdims=True)
            acc[:, cols] = alpha * acc[:, cols] + _dot(vt16, p.astype(BF16))
            m_scr[0:1, cols] = m_new

    @pl.when(ki < qi - 1)
    def _():
        step(None)

    @pl.when(ki == qi - 1)
    def _():
        step(1)

    @pl.when(ki == qi)
    def _():
        step(0)
        o = acc[...] / l_scr[0:1, :]
        lam = _diff_lambda(lp_ref[...], lam_init)
        a = o[:, 0:tq] - lam * o[:, tq:2 * tq]
        a = a * lax.rsqrt(jnp.mean(a * a, axis=0, keepdims=True) + EPS)
        o_ref[...] = (a.T * g_ref[...] * (1.0 - lam_init)).astype(o_ref.dtype)


def _attn_prompt(u_all, nb, t_len, rel_bias_flat, a_lambda, subln_g, layer, name):
    tq = _tile(t_len, 512, LANE)
    nq = t_len // tq
    lam_init = _lambda_init(layer)
    mp = nb * t_len
    vt = u_all[:mp, C_AV:C_AV + A_W].astype(BF16).reshape(nb, t_len, A_HEADS, A_DV)
    vt = vt.transpose(0, 2, 3, 1).reshape(nb * A_HEADS * A_DV, t_len)
    pairs = [(qi, ki) for qi in range(nq) for ki in range(qi + 1)]
    qi_of = jnp.array([p[0] for p in pairs], jnp.int32)
    ki_of = jnp.array([p[1] for p in pairs], jnp.int32)
    grid_spec = pltpu.PrefetchScalarGridSpec(
        num_scalar_prefetch=2,
        grid=(nb, A_HEADS, len(pairs)),
        in_specs=[
            pl.BlockSpec(memory_space=pltpu.SMEM),
            pl.BlockSpec((tq, A_DV), lambda b, h, p, qi, ki: (b * nq + qi[p], C_AQ // A_DV + h)),
            pl.BlockSpec((tq, A_DV), lambda b, h, p, qi, ki: (b * nq + ki[p], C_AK // A_DV + h)),
            pl.BlockSpec((A_DV, tq), lambda b, h, p, qi, ki: (b * A_HEADS + h, ki[p])),
            pl.BlockSpec((4, A_DQK), lambda b, h, p, qi, ki: (0, 0)),
            pl.BlockSpec((1, A_DV), lambda b, h, p, qi, ki: (0, 0)),
        ],
        out_specs=pl.BlockSpec((tq, A_DV), lambda b, h, p, qi, ki: (b * nq + qi[p], h)),
        scratch_shapes=[pltpu.VMEM((2 * tq, A_DV), BF16),
                        pltpu.VMEM((SUBLANE, 2 * tq), F32),
                        pltpu.VMEM((SUBLANE, 2 * tq), F32),
                        pltpu.VMEM((A_DV, 2 * tq), F32),
                        pltpu.VMEM((2, tq, tq), F32)],
    )
    return pl.pallas_call(
        functools.partial(_attn_prompt_kernel, tq=tq, lam_init=lam_init),
        out_shape=jax.ShapeDtypeStruct((mp, A_W), BF16),
        grid_spec=grid_spec,
        compiler_params=_params("arbitrary", "arbitrary", "arbitrary"),
        name=name,
    )(qi_of, ki_of, rel_bias_flat, u_all, u_all, vt, a_lambda, subln_g)


def _attn_sample_kernel(pt_ref, rb_ref, q_ref, kn_ref, vn_ref, lp_ref, g_ref, *rest,
                        n_pg, t_len, page, past, lam_init):
    kp = rest[0:n_pg]
    vp = rest[n_pg:2 * n_pg]
    o_ref = rest[2 * n_pg]
    qs, m_scr, l_scr, acc, knp, vnp = rest[2 * n_pg + 1:]
    s_id = pl.program_id(1)
    n_steps = pl.num_programs(1)
    n_cache = n_steps - 1
    rows = A_HEADS * 2 * t_len
    cols = page * A_HEADS
    row = lax.broadcasted_iota(jnp.int32, (rows, 1), 0)
    row_h = row // (2 * t_len)
    row_q = row % t_len

    def head_col(b):
        c = jnp.full((rows, 1), rb_ref[b], F32)
        for hh in range(1, A_HEADS):
            c = jnp.where(row_h == hh, rb_ref[hh * NUM_BUCKETS + b], c)
        return c

    def exact_bias(dist):
        bucket = _t5_bucket(dist)
        far = head_col(NUM_BUCKETS - 1)
        val = jnp.zeros(dist.shape, F32)
        for b in range(NUM_BUCKETS - 2, -1, -1):
            val = jnp.where(bucket == b, (head_col(b) - far) * LOG2E, val)
        return val

    @pl.when(s_id == 0)
    def _():
        q = q_ref[...] * ((A_DQK ** -0.5) * LOG2E)
        lane = lax.broadcasted_iota(jnp.int32, (t_len, A_DV), 1)
        for hh in range(A_HEADS):
            qh = q[:, hh * A_DV:(hh + 1) * A_DV]
            qs[hh * 2 * t_len:hh * 2 * t_len + t_len, :] = jnp.where(lane < A_DQK, qh, 0.0).astype(BF16)
            qs[hh * 2 * t_len + t_len:(hh + 1) * 2 * t_len, :] = jnp.where(lane >= A_DQK, qh, 0.0).astype(BF16)
        m_scr[...] = jnp.full(m_scr.shape, NEG, F32)
        l_scr[...] = jnp.zeros(l_scr.shape, F32)
        acc[...] = jnp.zeros(acc.shape, F32)

    def head_rows(refs, hh):
        return jnp.concatenate([r[pl.ds(hh, page, stride=A_HEADS), :].astype(BF16) for r in refs], axis=0)

    def cache_step(exact):
        n_keys = n_pg * page
        s_all = jnp.concatenate(
            [_dot_nt(qs[hh * 2 * t_len:(hh + 1) * 2 * t_len, :], head_rows(kp, hh)) for hh in range(A_HEADS)],
            axis=0)
        if exact:
            k_pos = (s_id - 1) * n_keys + lax.broadcasted_iota(jnp.int32, (rows, n_keys), 1)
            s_all = s_all + exact_bias(past + row_q - k_pos)

        def pv(p):
            return jnp.concatenate(
                [_dot(p[hh * 2 * t_len:(hh + 1) * 2 * t_len, :].astype(BF16), head_rows(vp, hh))
                 for hh in range(A_HEADS)], axis=0)

        _softmax_step(s_all, m_scr, l_scr, acc, pv)

    @pl.when((s_id >= 1) & (s_id < n_cache))
    def _():
        cache_step(False)

    @pl.when(s_id == n_cache)
    def _():
        cache_step(True)

    @pl.when(s_id == 0)
    def _():
        knp[...] = jnp.zeros(knp.shape, BF16)
        vnp[...] = jnp.zeros(vnp.shape, BF16)
        kn = kn_ref[...]
        vn = vn_ref[...]
        for hh in range(A_HEADS):
            knp[hh, 0:t_len, :] = kn[:, hh * A_DV:(hh + 1) * A_DV].astype(BF16)
            vnp[hh, 0:t_len, :] = vn[:, hh * A_DV:(hh + 1) * A_DV].astype(BF16)
        kcol = lax.broadcasted_iota(jnp.int32, (rows, page), 1)
        dist = row_q - kcol
        bias = jnp.where((dist >= 0) & (kcol < t_len), exact_bias(dist), NEG)
        parts = []
        for hh in range(A_HEADS):
            parts.append(_dot_nt(qs[hh * 2 * t_len:(hh + 1) * 2 * t_len, :], knp[hh]))
        s = jnp.concatenate(parts, axis=0) + bias

        def pv(p):
            outs = []
            for hh in range(A_HEADS):
                outs.append(_dot(p[hh * 2 * t_len:(hh + 1) * 2 * t_len, :].astype(BF16), vnp[hh]))
            return jnp.concatenate(outs, axis=0)

        _softmax_step(s, m_scr, l_scr, acc, pv)

    @pl.when(s_id == n_cache)
    def _():
        o = acc[...] / l_scr[...]
        lam = _diff_lambda(lp_ref[...], lam_init)
        g = g_ref[...]
        for hh in range(A_HEADS):
            r0 = hh * 2 * t_len
            a = o[r0:r0 + t_len, :] - lam * o[r0 + t_len:r0 + 2 * t_len, :]
            o_ref[:, hh * A_DV:(hh + 1) * A_DV] = (_rms(a, g) * (1.0 - lam_init)).astype(o_ref.dtype)


def _attn_sample(u_all, row0, nb, t_len, cache_k, cache_v, page_table, rel_bias_flat, a_lambda,
                 subln_g, layer, name):
    depth, n_pool, page = cache_k.shape[0], cache_k.shape[1], cache_k.shape[2]
    n_pages = page_table.shape[1]
    n_pg = _tile(n_pages, 16, 1)
    n_cache = n_pages // n_pg
    past = n_pages * page
    rb0 = row0 // t_len
    rows = A_HEADS * 2 * t_len
    ck = cache_k.reshape(depth, n_pool, page * A_HEADS, A_DV)
    cv = cache_v.reshape(depth, n_pool, page * A_HEADS, A_DV)
    pt = page_table.reshape(-1)

    def page_spec(i):
        def imap(b, s, pt_ref):
            step = jnp.maximum(s - 1, 0)
            return (layer, pt_ref[b * n_pages + step * n_pg + i], 0, 0)
        return pl.BlockSpec((None, None, page * A_HEADS, A_DV), imap)

    grid_spec = pltpu.PrefetchScalarGridSpec(
        num_scalar_prefetch=1,
        grid=(nb, n_cache + 1),
        in_specs=[
            pl.BlockSpec(memory_space=pltpu.SMEM),
            pl.BlockSpec((t_len, A_W), lambda b, s, pt_ref: (rb0 + b, C_AQ // A_W)),
            pl.BlockSpec((t_len, A_W), lambda b, s, pt_ref: (rb0 + b, C_AK // A_W)),
            pl.BlockSpec((t_len, A_W), lambda b, s, pt_ref: (rb0 + b, C_AV // A_W)),
            pl.BlockSpec((4, A_DQK), lambda b, s, pt_ref: (0, 0)),
            pl.BlockSpec((1, A_DV), lambda b, s, pt_ref: (0, 0)),
        ] + [page_spec(i) for i in range(n_pg)] + [page_spec(i) for i in range(n_pg)],
        out_specs=pl.BlockSpec((t_len, A_W), lambda b, s, pt_ref: (b, 0)),
        scratch_shapes=[pltpu.VMEM((rows, A_DV), BF16),
                        pltpu.VMEM((rows, 1), F32),
                        pltpu.VMEM((rows, 1), F32),
                        pltpu.VMEM((rows, A_DV), F32),
                        pltpu.VMEM((A_HEADS, page, A_DV), BF16),
                        pltpu.VMEM((A_HEADS, page, A_DV), BF16)],
    )
    return pl.pallas_call(
        functools.partial(_attn_sample_kernel, n_pg=n_pg, t_len=t_len, page=page, past=past,
                          lam_init=_lambda_init(layer)),
        out_shape=jax.ShapeDtypeStruct((nb * t_len, A_W), BF16),
        grid_spec=grid_spec,
        compiler_params=_params("arbitrary", "arbitrary"),
        name=name,
    )(pt, rel_bias_flat, u_all, u_all, u_all, a_lambda, subln_g, *([ck] * n_pg), *([cv] * n_pg))


def _merge_kernel(x_ref, og_ref, oa_ref, ol_ref, gt_ref, wg_ref, wa_ref, wl_ref, wo_ref, o_ref):
    d = x_ref.shape[1]
    gt = gt_ref[...].astype(F32)
    m = gt[:, 0:d] * _dot(og_ref[...], wg_ref[...])
    m = m + gt[:, d:2 * d] * _dot(oa_ref[...], wa_ref[...])
    m = m + gt[:, 2 * d:3 * d] * _dot(ol_ref[...], wl_ref[...])
    o_ref[...] = x_ref[...] + _dot(m.astype(BF16), wo_ref[...])


def _merge(x, o_g, o_a, o_l, gates, wg, wa, wl, wo, name):
    m, d = x.shape
    tm = _tile(m, 640)
    row = lambda w: pl.BlockSpec((tm, w), lambda i: (i, 0))
    full = lambda a: pl.BlockSpec(a.shape, lambda i: (0, 0))
    return pl.pallas_call(
        _merge_kernel,
        out_shape=jax.ShapeDtypeStruct((m, d), F32),
        grid=(m // tm,),
        in_specs=[row(d), row(G_W), row(A_W), row(L_WIDTH), row(3 * d),
                  full(wg), full(wa), full(wl), full(wo)],
        out_specs=row(d),
        compiler_params=_params("parallel"),
        name=name,
    )(x, o_g, o_a, o_l, gates, wg, wa, wl, wo)


def _ffn_kernel(x_ref, g_ref, w1_ref, w3_ref, w2_ref, o_ref, h_scr, acc):
    f = pl.program_id(1)

    @pl.when(f == 0)
    def _():
        x = x_ref[...]
        h_scr[...] = _rms(x, g_ref[...]).astype(BF16)
        acc[...] = x

    h = h_scr[...]
    t = _silu(_dot(h, w1_ref[...])) * _dot(h, w3_ref[...])
    acc[...] += _dot(t.astype(BF16), w2_ref[...])

    @pl.when(f == pl.num_programs(1) - 1)
    def _():
        o_ref[...] = acc[...]


def _ffn(x, g, w1, w3, w2, name):
    m, d = x.shape
    ff = w1.shape[1]
    tm = _tile(m, 1280)
    tf = _tile(ff, 512, LANE)
    return pl.pallas_call(
        _ffn_kernel,
        out_shape=jax.ShapeDtypeStruct((m, d), F32),
        grid=(m // tm, ff // tf),
        in_specs=[
            pl.BlockSpec((tm, d), lambda i, f: (i, 0)),
            pl.BlockSpec((1, d), lambda i, f: (0, 0)),
            pl.BlockSpec((d, tf), lambda i, f: (0, f)),
            pl.BlockSpec((d, tf), lambda i, f: (0, f)),
            pl.BlockSpec((tf, d), lambda i, f: (f, 0)),
        ],
        out_specs=pl.BlockSpec((tm, d), lambda i, f: (i, 0)),
        scratch_shapes=[pltpu.VMEM((tm, d), BF16), pltpu.VMEM((tm, d), F32)],
        compiler_params=_params("parallel", "arbitrary"),
        name=name,
    )(x, g, w1, w3, w2)


def _router_kernel(x_ref, g_ref, wr_ref, o_ref):
    h = _rms(x_ref[...], g_ref[...])
    wr_hi, wr_lo = _split(wr_ref[...])
    logits = _dot_split(h, wr_hi, wr_lo)
    lane = lax.broadcasted_iota(jnp.int32, logits.shape, 1)
    logits = jnp.where(lane < N_EXPERTS, logits, -jnp.inf)
    m1 = jnp.max(logits, axis=-1, keepdims=True)
    i1 = jnp.min(jnp.where(logits == m1, lane, LANE), axis=-1, keepdims=True)
    rest = jnp.where(lane == i1, -jnp.inf, logits)
    m2 = jnp.max(rest, axis=-1, keepdims=True)
    i2 = jnp.min(jnp.where(rest == m2, lane, LANE), axis=-1, keepdims=True)
    e2 = jnp.exp(m2 - m1)
    den = 1.0 + e2
    o_ref[...] = jnp.where(lane == i1, 1.0 / den, 0.0) + jnp.where(lane == i2, e2 / den, 0.0)


def _router(x, g, wr, name):
    m, d = x.shape
    tm = _tile(m, 1280)
    return pl.pallas_call(
        _router_kernel,
        out_shape=jax.ShapeDtypeStruct((m, LANE), F32),
        grid=(m // tm,),
        in_specs=[pl.BlockSpec((tm, d), lambda i: (i, 0)),
                  pl.BlockSpec((1, d), lambda i: (0, 0)),
                  pl.BlockSpec((d, LANE), lambda i: (0, 0))],
        out_specs=pl.BlockSpec((tm, LANE), lambda i: (i, 0)),
        compiler_params=_params("parallel"),
        name=name,
    )(x, g, wr)


def _moe_kernel(x_ref, g_ref, gate_ref, w1_ref, w3_ref, w2_ref, fg_ref, o_ref, h_scr, acc):
    e = pl.program_id(1)
    f = pl.program_id(2)
    first = (e == 0) & (f == 0)
    last = (e == pl.num_programs(1) - 1) & (f == pl.num_programs(2) - 1)

    @pl.when(first)
    def _():
        x = x_ref[...]
        h_scr[...] = _rms(x, g_ref[...]).astype(BF16)
        acc[...] = x

    gate = gate_ref[...]
    lane = lax.broadcasted_iota(jnp.int32, gate.shape, 1)
    ge = jnp.sum(jnp.where(lane == e, gate, 0.0), axis=-1, keepdims=True)
    h = h_scr[...]
    t = _silu(_dot(h, w1_ref[...])) * _dot(h, w3_ref[...])
    acc[...] += ge * _dot(t.astype(BF16), w2_ref[...])

    @pl.when(last)
    def _():
        o_ref[...] = _rms(acc[...], fg_ref[...])


def _moe(x, g, gate, w1, w3, w2, final_g, name):
    m, d = x.shape
    ne, _, ff = w1.shape
    tm = _tile(m, 1280)
    tf = _tile(ff, 512, LANE)
    return pl.pallas_call(
        _moe_kernel,
        out_shape=jax.ShapeDtypeStruct((m, d), F32),
        grid=(m // tm, ne, ff // tf),
        in_specs=[
            pl.BlockSpec((tm, d), lambda i, e, f: (i, 0)),
            pl.BlockSpec((1, d), lambda i, e, f: (0, 0)),
            pl.BlockSpec((tm, LANE), lambda i, e, f: (i, 0)),
            pl.BlockSpec((None, d, tf), lambda i, e, f: (e, 0, f)),
            pl.BlockSpec((None, d, tf), lambda i, e, f: (e, 0, f)),
            pl.BlockSpec((None, tf, d), lambda i, e, f: (e, f, 0)),
            pl.BlockSpec((1, d), lambda i, e, f: (0, 0)),
        ],
        out_specs=pl.BlockSpec((tm, d), lambda i, e, f: (i, 0)),
        scratch_shapes=[pltpu.VMEM((tm, d), BF16), pltpu.VMEM((tm, d), F32)],
        compiler_params=_params("parallel", "arbitrary", "arbitrary"),
        name=name,
    )(x, g, gate, w1, w3, w2, final_g)


def _final_norm_kernel(x_ref, g_ref, o_ref):
    o_ref[...] = _rms(x_ref[...], g_ref[...])


def _final_norm(x, g, name):
    m, d = x.shape
    tm = _tile(m, 1280)
    return pl.pallas_call(
        _final_norm_kernel,
        out_shape=jax.ShapeDtypeStruct((m, d), F32),
        grid=(m // tm,),
        in_specs=[pl.BlockSpec((tm, d), lambda i: (i, 0)), pl.BlockSpec((1, d), lambda i: (0, 0))],
        out_specs=pl.BlockSpec((tm, d), lambda i: (i, 0)),
        compiler_params=_params("parallel"),
        name=name,
    )(x, g)


def _split_w_in(w):
    o_gb = 4 * G_W
    o_aq = o_gb + 2 * G_HEADS
    pad = jnp.zeros((w.shape[0], LANE - 2 * G_HEADS), w.dtype)
    return (jnp.concatenate([w[:, :o_gb], w[:, o_aq:]], axis=1),
            jnp.concatenate([w[:, o_gb:o_aq], pad], axis=1))


def _block_diag(w):
    n, d, _ = w.shape
    eye = jnp.eye(n, dtype=w.dtype)
    return (eye[:, None, :, None] * w[:, :, None, :]).reshape(n * d, n * d)


def kernel(x_prompt, x_sample, cache_k, cache_v, page_table, state_gdn, state_gdn_conv, state_lru, state_lru_conv, norm1_g, norm2_g, final_g, w_in, w_gate, b_gate, g_conv_w, g_a_log, g_dt_bias, g_norm_g, a_lambda, a_subln_g, rel_bias, l_conv_w, l_conv_b, l_wa, l_ba, l_wx, l_bx, l_lambda, w_br_g, w_br_a, w_br_l, w_out, f_w1, f_w3, f_w2, m_router, m_w1, m_w3, m_w2):
    bp, tp, d = x_prompt.shape
    bs, ts, _ = x_sample.shape
    depth = w_in.shape[0]
    mp = bp * tp
    ms = bs * ts
    x = jnp.concatenate([x_prompt.reshape(mp, d), x_sample.reshape(ms, d)], axis=0)
    rb_flat = rel_bias.T.reshape(-1)
    row = lambda v: v.reshape(1, -1)

    k_p, v_p, k_s, v_s = [], [], [], []
    sg_p, sg_s, cg_p, cg_s, sl_p, sl_s, cl_p, cl_s = [], [], [], [], [], [], [], []
    y = None
    for l in range(depth):
        w_main, w_gba = _split_w_in(w_in[l])
        u = _norm_matmul(x, row(norm1_g[l]), w_main.astype(BF16), jnp.zeros((1, U_COLS), F32),
                         None, F32, f"in_proj_{l}")
        gba = _norm_matmul(x, row(norm1_g[l]), w_gba.astype(BF16), jnp.zeros((1, LANE), F32),
                           None, F32, f"in_proj_gba_{l}")
        gates = _norm_matmul(x, row(norm1_g[l]), w_gate[l].astype(BF16), row(b_gate[l]),
                             "sigmoid", BF16, f"gate_proj_{l}")

        ab = jnp.zeros((2, LANE), F32)
        ab = ab.at[0, G_HEADS:2 * G_HEADS].set(g_a_log[l]).at[1, G_HEADS:2 * G_HEADS].set(g_dt_bias[l])
        ng = row(g_norm_g[l])
        og_p, s_p = _gdn(u, gba, 0, bp, tp, jnp.zeros((bp, CONV_W - 1, 3 * G_W), F32),
                         jnp.zeros((bp, G_HEADS, G_DK, G_DV), F32), g_conv_w[l], ab, ng, f"gdn_prompt_{l}")
        og_s, s_s = _gdn(u, gba, mp, bs, ts, state_gdn_conv[l], state_gdn[l], g_conv_w[l], ab, ng,
                         f"gdn_sample_{l}")

        sub_g = row(a_subln_g[l])
        oa_p = _attn_prompt(u, bp, tp, rb_flat, a_lambda[l], sub_g, l, f"attn_prompt_{l}")
        oa_s = _attn_sample(u, mp, bs, ts, cache_k, cache_v, page_table, rb_flat, a_lambda[l], sub_g, l,
                            f"attn_sample_{l}")

        wa_bd = _block_diag(l_wa[l]).astype(BF16)
        wx_bd = _block_diag(l_wx[l]).astype(BF16)
        lru_args = (l_conv_w[l], row(l_conv_b[l]), wa_bd, row(l_ba[l]), wx_bd, row(l_bx[l]), row(l_lambda[l]))
        ol_p, h_p = _lru(u, 0, bp, tp, jnp.zeros((bp, CONV_W - 1, L_WIDTH), F32),
                         jnp.zeros((bp, L_WIDTH), F32), *lru_args, f"lru_prompt_{l}")
        ol_s, h_s = _lru(u, mp, bs, ts, state_lru_conv[l], state_lru[l], *lru_args, f"lru_sample_{l}")

        x = _merge(x, jnp.concatenate([og_p, og_s]), jnp.concatenate([oa_p, oa_s]),
                   jnp.concatenate([ol_p, ol_s]), gates, w_br_g[l].astype(BF16), w_br_a[l].astype(BF16),
                   w_br_l[l].astype(BF16), w_out[l].astype(BF16), f"merge_{l}")

        j = l // 2
        if l % 2 == 0:
            x = _ffn(x, row(norm2_g[l]), f_w1[j].astype(BF16), f_w3[j].astype(BF16), f_w2[j].astype(BF16),
                     f"ffn_{l}")
            if l == depth - 1:
                y = _final_norm(x, row(final_g), "final_norm")
        else:
            wr = jnp.zeros((d, LANE), F32).at[:, :N_EXPERTS].set(m_router[j])
            gate = _router(x, row(norm2_g[l]), wr, f"router_{l}")
            fg = row(final_g) if l == depth - 1 else None
            if fg is None:
                raise NotImplementedError("mixture layer is only fused with the final norm")
            y = _moe(x, row(norm2_g[l]), gate, m_w1[j].astype(BF16), m_w3[j].astype(BF16),
                     m_w2[j].astype(BF16), fg, f"moe_{l}")

        def tail_rows(c0, width):
            keep = CONV_W - 1
            p = jnp.stack([u[(b + 1) * tp - keep:(b + 1) * tp, c0:c0 + width] for b in range(bp)])
            s = u[mp:, c0:c0 + width].reshape(bs, ts, width)[:, ts - keep:]
            return p, s

        k_p.append(u[:mp, C_AK:C_AK + A_W].reshape(bp, tp, A_HEADS, A_DV))
        v_p.append(u[:mp, C_AV:C_AV + A_W].reshape(bp, tp, A_HEADS, A_DV))
        k_s.append(u[mp:, C_AK:C_AK + A_W].reshape(bs, ts, A_HEADS, A_DV))
        v_s.append(u[mp:, C_AV:C_AV + A_W].reshape(bs, ts, A_HEADS, A_DV))
        sg_p.append(s_p)
        sg_s.append(s_s)
        cp, cs = tail_rows(C_GQKV, 3 * G_W)
        cg_p.append(cp)
        cg_s.append(cs)
        sl_p.append(h_p)
        sl_s.append(h_s)
        cp, cs = tail_rows(C_LX, L_WIDTH)
        cl_p.append(cp)
        cl_s.append(cs)

    st = jnp.stack
    return (y[:mp].reshape(bp, tp, d), y[mp:].reshape(bs, ts, d),
            st(k_p), st(v_p), st(k_s), st(v_s), st(sg_p), st(sg_s), st(cg_p), st(cg_s),
            st(sl_p), st(sl_s), st(cl_p), st(cl_s))
```

```python
import functools
import math

import jax
import jax.numpy as jnp
from jax import lax
from jax.experimental import pallas as pl
from jax.experimental.pallas import tpu as pltpu

F32 = jnp.float32
BF16 = jnp.bfloat16
EPS = 1e-6
NEG = -1e30

G_HEADS, G_DK, G_DV = 4, 128, 128
G_W = G_HEADS * G_DK
GDN_CHUNK = 64
CONV_W = 4
A_HEADS, A_DQK, A_DV = 4, 64, 128
A_W = A_HEADS * A_DV
L_WIDTH, L_BLOCKS = 512, 8
LRU_C = 8.0
NUM_BUCKETS, MAX_DISTANCE = 32, 128
N_EXPERTS, TOP_K = 8, 2
LANE = 128
SUBLANE = 8
VMEM_LIMIT = 56 * 1024 * 1024

C_GQKV = 0
C_GZ = 3 * G_W
C_AQ = 4 * G_W
C_AK = C_AQ + A_W
C_AV = C_AK + A_W
C_LX = C_AV + A_W
C_LY = C_LX + L_WIDTH
U_COLS = C_LY + L_WIDTH


def _tile(n, target, mult=SUBLANE):
    best = None
    for d in range(mult, min(n, target) + 1, mult):
        if n % d == 0:
            best = d
    return best if best is not None else n


def _params(*sem):
    return pltpu.CompilerParams(dimension_semantics=sem, vmem_limit_bytes=VMEM_LIMIT)


def _dot(a, b):
    return jnp.dot(a, b, preferred_element_type=F32)


def _dot_nt(a, b):
    return lax.dot_general(a, b, (((1,), (1,)), ((), ())), preferred_element_type=F32)


def _dot_tn(a, b):
    return lax.dot_general(a, b, (((0,), (0,)), ((), ())), preferred_element_type=F32)


def _split(x):
    hi = x.astype(BF16)
    return hi, (x - hi.astype(F32)).astype(BF16)


def _dot_split(a, b_hi, b_lo):
    a_hi, a_lo = _split(a)
    return _dot(a_hi, b_hi) + (_dot(a_hi, b_lo) + _dot(a_lo, b_hi))


def _dot_lhs_exact(a16, b):
    b_hi, b_lo = _split(b)
    return _dot(a16, b_hi) + _dot(a16, b_lo)


def _sigmoid(x):
    return 1.0 / (1.0 + jnp.exp(-x))


def _silu(x):
    return x * _sigmoid(x)


def _softplus(x):
    return jnp.maximum(x, 0.0) + jnp.log1p(jnp.exp(-jnp.abs(x)))


def _rms(x, g):
    return x * lax.rsqrt(jnp.mean(x * x, axis=-1, keepdims=True) + EPS) * g


def _norm_mm_kernel(x_ref, g_ref, w_ref, b_ref, o_ref, h_scr, *, act):
    @pl.when(pl.program_id(1) == 0)
    def _():
        h_scr[...] = _rms(x_ref[...], g_ref[...]).astype(BF16)

    acc = _dot(h_scr[...], w_ref[...]) + b_ref[...]
    if act == "sigmoid":
        acc = _sigmoid(acc)
    o_ref[...] = acc.astype(o_ref.dtype)


def _norm_matmul(x, g, w, b, act, out_dtype, name):
    m, k = x.shape
    n = w.shape[1]
    tm = _tile(m, 1280)
    tn = _tile(n, 512, LANE)
    return pl.pallas_call(
        functools.partial(_norm_mm_kernel, act=act),
        out_shape=jax.ShapeDtypeStruct((m, n), out_dtype),
        grid=(m // tm, n // tn),
        in_specs=[
            pl.BlockSpec((tm, k), lambda i, j: (i, 0)),
            pl.BlockSpec((1, k), lambda i, j: (0, 0)),
            pl.BlockSpec((k, tn), lambda i, j: (0, j)),
            pl.BlockSpec((1, tn), lambda i, j: (0, j)),
        ],
        out_specs=pl.BlockSpec((tm, tn), lambda i, j: (i, j)),
        scratch_shapes=[pltpu.VMEM((tm, k), BF16)],
        compiler_params=_params("parallel", "arbitrary"),
        name=name,
    )(x, g, w, b)


def _gdn_kernel(qkv_ref, z_ref, gba_ref, cs_ref, s0_ref, cw_ref, ab_ref, ng_ref,
                o_ref, sout_ref, xe, ys, bg, s_scr, *, tt, chunk, rows):
    t = pl.program_id(1)
    nt = pl.num_programs(1)
    hc = G_HEADS * chunk

    @pl.when(t == 0)
    def _():
        for h in range(G_HEADS):
            s_scr[:, h * G_DV:(h + 1) * G_DV] = s0_ref[h]
        xe[...] = jnp.zeros(xe.shape, F32)
        xe[SUBLANE - (CONV_W - 1):SUBLANE, :] = cs_ref[...]

    xe[SUBLANE:SUBLANE + tt, :] = qkv_ref[...]
    cw = cw_ref[...]
    base = SUBLANE - (CONV_W - 1)
    y = xe[base:base + rows, :] * cw[0:1, :]
    for j in range(1, CONV_W):
        y = y + xe[base + j:base + j + rows, :] * cw[j:j + 1, :]
    xe[0:SUBLANE, :] = xe[tt:tt + SUBLANE, :]

    gba = gba_ref[...]
    if rows > tt:
        gba = jnp.concatenate([gba, jnp.zeros((rows - tt, LANE), F32)], axis=0)
    ab = ab_ref[...]
    lane_g = lax.broadcasted_iota(jnp.int32, (rows, LANE), 1)
    bgc = jnp.where(lane_g < G_HEADS, _sigmoid(gba), -jnp.exp(ab[0:1, :]) * _softplus(gba + ab[1:2, :]))
    y = _silu(y)
    if rows > tt:
        live = lax.broadcasted_iota(jnp.int32, (rows, 1), 0) < tt
        y = jnp.where(live, y, 0.0)
        bgc = jnp.where(live, bgc, 0.0)
    bg[...] = bgc
    for h in range(G_HEADS):
        q = y[:, h * G_DK:(h + 1) * G_DK]
        k = y[:, G_W + h * G_DK:G_W + (h + 1) * G_DK]
        ys[:, h * G_DK:(h + 1) * G_DK] = q * lax.rsqrt(jnp.sum(q * q, axis=-1, keepdims=True) + 1e-6) * (G_DK ** -0.5)
        ys[:, G_W + h * G_DK:G_W + (h + 1) * G_DK] = k * lax.rsqrt(jnp.sum(k * k, axis=-1, keepdims=True) + 1e-6)
    ys[:, 2 * G_W:3 * G_W] = y[:, 2 * G_W:3 * G_W]

    def iota(shape, d):
        return lax.broadcasted_iota(jnp.int32, shape, d)

    r_c = iota((chunk, hc), 0)
    j_c = iota((chunk, hc), 1) % chunk
    tril_cat = r_c >= j_c
    strict_cat = r_c > j_c
    eye_cat = (r_c == j_c).astype(F32)
    triu_cat = (r_c <= j_c).astype(F32)
    tril_f = jnp.where(iota((chunk, chunk), 0) >= iota((chunk, chunk), 1), 1.0, 0.0).astype(BF16)
    ones_cc = jnp.ones((chunk, chunk), BF16)
    l_s = iota((LANE, hc + 2 * G_W), 0)
    n_s = iota((LANE, hc + 2 * G_W), 1)
    src_lane = jnp.where(n_s < hc, G_HEADS + n_s // chunk,
                         jnp.where(n_s < hc + G_W, G_HEADS + (n_s - hc) // G_DK,
                                   (n_s - hc - G_W) // G_DK))
    sel = jnp.where(l_s == src_lane, 1.0, 0.0).astype(BF16)
    bd_cc = iota((hc, hc), 0) // chunk == iota((hc, hc), 1) // chunk
    bd_cd = iota((hc, G_W), 0) // chunk == iota((hc, G_W), 1) // G_DK
    bd_dd = iota((G_W, G_W), 0) // G_DK == iota((G_W, G_W), 1) // G_DK
    ng = ng_ref[...]
    n_doubling = int(math.log2(chunk)) - 1

    def blockdiag(x, mask):
        reps = mask.shape[0] // x.shape[0]
        return jnp.where(mask, jnp.concatenate([x] * reps, axis=0), 0.0)

    def phase_a(c0):
        bg_hi, bg_lo = _split(bg[c0:c0 + chunk, :])
        selm = _dot(bg_hi, sel) + _dot(bg_lo, sel)
        cum = _dot_lhs_exact(tril_f, selm[:, 0:hc + G_W])
        gi_cat = cum[:, 0:hc]
        gi = cum[:, hc:hc + G_W]
        gj_cat = _dot_lhs_exact(ones_cc, selm[:, 0:hc] * triu_cat)
        beta = selm[:, hc + G_W:hc + 2 * G_W]
        decay = jnp.where(tril_cat, jnp.exp(gi_cat - gj_cat), 0.0)
        egi = jnp.exp(gi)
        glast = gi[chunk - 1:chunk, :]
        q = ys[c0:c0 + chunk, 0:G_W]
        k = ys[c0:c0 + chunk, G_W:2 * G_W]
        v = ys[c0:c0 + chunk, 2 * G_W:3 * G_W]
        kb = k * beta
        bdk = blockdiag(k, bd_cd).astype(BF16)
        la = _dot_nt(jnp.concatenate([kb, q], axis=0).astype(BF16), bdk)
        lmat = jnp.where(strict_cat, la[0:chunk] * decay, 0.0)
        a_in = la[chunk:2 * chunk] * decay
        pw = -lmat
        tinv = eye_cat + pw
        bd_hi, bd_lo = _split(blockdiag(pw, bd_cc))
        for _ in range(n_doubling):
            pw = _dot_split(pw, bd_hi, bd_lo)
            bd_hi, bd_lo = _split(blockdiag(pw, bd_cc))
            tinv = tinv + _dot_split(tinv, bd_hi, bd_lo)
        rhs = jnp.concatenate([blockdiag(v * beta, bd_cd), blockdiag(kb * egi, bd_cd)], axis=1)
        uw = _dot(tinv.astype(BF16), rhs.astype(BF16))
        return dict(u=uw[:, 0:G_W], w=uw[:, G_W:2 * G_W], qe=q * egi, a=a_in.astype(BF16),
                    kd=(k * jnp.exp(glast - gi)).astype(BF16), gl=jnp.exp(glast))

    def phase_b(c0, a):
        s_cat = s_scr[...]
        bds = blockdiag(s_cat, bd_dd).astype(BF16)
        ws = _dot(jnp.concatenate([a["w"], a["qe"]], axis=0).astype(BF16), bds)
        v_new = a["u"] - ws[0:chunk]
        o = ws[chunk:2 * chunk] + _dot(a["a"], blockdiag(v_new, bd_cd).astype(BF16))
        v16 = v_new.astype(BF16)
        n_out = min(tt, chunk)
        for h in range(G_HEADS):
            hs = slice(h * G_DV, (h + 1) * G_DV)
            s_scr[:, hs] = s_cat[:, hs] * a["gl"][:, hs] + _dot_tn(a["kd"][:, hs], v16[:, hs])
            zz = z_ref[c0:c0 + n_out, hs]
            o_ref[c0:c0 + n_out, hs] = (_rms(o[0:n_out, hs], ng) * _silu(zz)).astype(o_ref.dtype)

    n_chunks = rows // chunk
    nxt = phase_a(0)
    for c in range(n_chunks):
        cur = nxt
        if c + 1 < n_chunks:
            nxt = phase_a((c + 1) * chunk)
        phase_b(c * chunk, cur)

    @pl.when(t == nt - 1)
    def _():
        for h in range(G_HEADS):
            sout_ref[h] = s_scr[:, h * G_DV:(h + 1) * G_DV]


def _gdn(u_all, gba_all, row0, nb, t_len, conv_state, s0, conv_w, ab, norm_g, name):
    chunk = GDN_CHUNK
    if t_len >= chunk:
        tt = _tile(t_len, 256, chunk)
        rows = tt
    else:
        tt = t_len
        rows = chunk
    nt = t_len // tt
    rb0 = row0 // tt
    w3 = 3 * G_W
    return pl.pallas_call(
        functools.partial(_gdn_kernel, tt=tt, chunk=chunk, rows=rows),
        out_shape=(jax.ShapeDtypeStruct((nb * t_len, G_W), BF16),
                   jax.ShapeDtypeStruct((nb, G_HEADS, G_DK, G_DV), F32)),
        grid=(nb, nt),
        in_specs=[
            pl.BlockSpec((tt, w3), lambda b, t: (rb0 + b * nt + t, C_GQKV // w3)),
            pl.BlockSpec((tt, G_W), lambda b, t: (rb0 + b * nt + t, C_GZ // G_W)),
            pl.BlockSpec((tt, LANE), lambda b, t: (rb0 + b * nt + t, 0)),
            pl.BlockSpec((None, CONV_W - 1, w3), lambda b, t: (b, 0, 0)),
            pl.BlockSpec((None, G_HEADS, G_DK, G_DV), lambda b, t: (b, 0, 0, 0)),
            pl.BlockSpec((CONV_W, w3), lambda b, t: (0, 0)),
            pl.BlockSpec((2, LANE), lambda b, t: (0, 0)),
            pl.BlockSpec((1, G_DV), lambda b, t: (0, 0)),
        ],
        out_specs=(pl.BlockSpec((tt, G_W), lambda b, t: (b * nt + t, 0)),
                   pl.BlockSpec((None, G_HEADS, G_DK, G_DV), lambda b, t: (b, 0, 0, 0))),
        scratch_shapes=[pltpu.VMEM((SUBLANE + rows, w3), F32),
                        pltpu.VMEM((rows, w3), F32),
                        pltpu.VMEM((rows, LANE), F32),
                        pltpu.VMEM((G_DK, G_HEADS * G_DV), F32)],
        compiler_params=_params("parallel", "arbitrary"),
        name=name,
    )(u_all, u_all, gba_all, conv_state, s0, conv_w, ab, norm_g)


def _gelu_tanh(x):
    return 0.5 * x * (1.0 + jnp.tanh(math.sqrt(2.0 / math.pi) * (x + 0.044715 * (x * x * x))))


def _lru_kernel(lx_ref, ly_ref, cs_ref, h0_ref, cw_ref, cb_ref, wa_ref, ba_ref, wx_ref, bx_ref,
                lam_ref, o_ref, hout_ref, xe, h_scr, *, tt):
    t = pl.program_id(1)
    nt = pl.num_programs(1)

    @pl.when(t == 0)
    def _():
        h_scr[...] = jnp.broadcast_to(h0_ref[...], h_scr.shape)
        xe[0:SUBLANE, :] = jnp.zeros((SUBLANE, L_WIDTH), F32)
        xe[SUBLANE - (CONV_W - 1):SUBLANE, :] = cs_ref[...]

    xe[SUBLANE:SUBLANE + tt, :] = lx_ref[...]
    cw = cw_ref[...]
    base = SUBLANE - (CONV_W - 1)
    xc = xe[base:base + tt, :] * cw[0:1, :]
    for j in range(1, CONV_W):
        xc = xc + xe[base + j:base + j + tt, :] * cw[j:j + 1, :]
    xe[0:SUBLANE, :] = xe[tt:tt + SUBLANE, :]
    xc = xc + cb_ref[...]
    x16 = xc.astype(BF16)
    r = _sigmoid(_dot(x16, wa_ref[...]) + ba_ref[...])
    i = _sigmoid(_dot(x16, wx_ref[...]) + bx_ref[...])
    log_a = -LRU_C * r * _softplus(-lam_ref[...])
    a = jnp.exp(log_a)
    b = jnp.sqrt(jnp.tanh(-log_a) * (a * a + 1.0)) * (i * xc)
    row = lax.broadcasted_iota(jnp.int32, (tt, 1), 0)
    s = 1
    while s < tt:
        keep = row >= s
        a_prev = jnp.where(keep, pltpu.roll(a, s, 0), 1.0)
        b_prev = jnp.where(keep, pltpu.roll(b, s, 0), 0.0)
        b = a * b_prev + b
        a = a * a_prev
        s *= 2
    h = a * h_scr[0:1, :] + b
    h_last = h[tt - 1:tt, :]
    h_scr[...] = jnp.broadcast_to(h_last, h_scr.shape)
    o_ref[...] = (h * _gelu_tanh(ly_ref[...])).astype(o_ref.dtype)

    @pl.when(t == nt - 1)
    def _():
        hout_ref[...] = h_last


def _lru(u_all, row0, nb, t_len, conv_state, h0, cw, cb, wa, ba, wx, bx, lam, name):
    tt = _tile(t_len, 256)
    nt = t_len // tt
    rb0 = row0 // tt
    vec = lambda: pl.BlockSpec((1, L_WIDTH), lambda b, t: (0, 0))
    out, h_last = pl.pallas_call(
        functools.partial(_lru_kernel, tt=tt),
        out_shape=(jax.ShapeDtypeStruct((nb * t_len, L_WIDTH), BF16),
                   jax.ShapeDtypeStruct((nb, 1, L_WIDTH), F32)),
        grid=(nb, nt),
        in_specs=[
            pl.BlockSpec((tt, L_WIDTH), lambda b, t: (rb0 + b * nt + t, C_LX // L_WIDTH)),
            pl.BlockSpec((tt, L_WIDTH), lambda b, t: (rb0 + b * nt + t, C_LY // L_WIDTH)),
            pl.BlockSpec((None, CONV_W - 1, L_WIDTH), lambda b, t: (b, 0, 0)),
            pl.BlockSpec((None, 1, L_WIDTH), lambda b, t: (b, 0, 0)),
            pl.BlockSpec((CONV_W, L_WIDTH), lambda b, t: (0, 0)),
            vec(),
            pl.BlockSpec((L_WIDTH, L_WIDTH), lambda b, t: (0, 0)),
            vec(),
            pl.BlockSpec((L_WIDTH, L_WIDTH), lambda b, t: (0, 0)),
            vec(),
            vec(),
        ],
        out_specs=(pl.BlockSpec((tt, L_WIDTH), lambda b, t: (b * nt + t, 0)),
                   pl.BlockSpec((None, 1, L_WIDTH), lambda b, t: (b, 0, 0))),
        scratch_shapes=[pltpu.VMEM((SUBLANE + tt, L_WIDTH), F32),
                        pltpu.VMEM((SUBLANE, L_WIDTH), F32)],
        compiler_params=_params("parallel", "arbitrary"),
        name=name,
    )(u_all, u_all, conv_state, h0.reshape(nb, 1, L_WIDTH), cw, cb, wa, ba, wx, bx, lam)
    return out, h_last.reshape(nb, L_WIDTH)


def _t5_bucket(dist):
    n = jnp.maximum(dist, 0)
    max_exact = NUM_BUCKETS // 2
    nf = jnp.maximum(n, 1).astype(F32)
    large = max_exact + (jnp.log(nf / max_exact) / math.log(MAX_DISTANCE / max_exact)
                         * (NUM_BUCKETS - max_exact)).astype(jnp.int32)
    large = jnp.minimum(large, NUM_BUCKETS - 1)
    return jnp.where(n < max_exact, n, large)


def _lambda_init(layer):
    return 0.8 - 0.6 * math.exp(-0.3 * layer)


def _diff_lambda(lp, lam_init):
    s1 = jnp.sum(lp[0:1, :] * lp[1:2, :], axis=-1, keepdims=True)
    s2 = jnp.sum(lp[2:3, :] * lp[3:4, :], axis=-1, keepdims=True)
    return jnp.exp(s1) - jnp.exp(s2) + lam_init


def _softmax_step(s, m_ref, l_ref, acc_ref, pv):
    m_prev = m_ref[...]
    m_new = jnp.maximum(m_prev, jnp.max(s, axis=-1, keepdims=True))
    alpha = jnp.exp2(m_prev - m_new)
    p = jnp.exp2(s - m_new)
    l_ref[...] = alpha * l_ref[...] + jnp.sum(p, axis=-1, keepdims=True)
    acc_ref[...] = alpha * acc_ref[...] + pv(p)
    m_ref[...] = m_new


LOG2E = math.log2(math.e)
ATTN_COL_CHUNK = LANE


def _attn_prompt_kernel(qi_ref, ki_ref, rb_ref, q_ref, k_ref, vt_ref, lp_ref, g_ref, o_ref,
                        qs, m_scr, l_scr, acc, tab, *, tq, lam_init):
    h = pl.program_id(1)
    qi = qi_ref[pl.program_id(2)]
    ki = ki_ref[pl.program_id(2)]
    cw = min(ATTN_COL_CHUNK, tq)

    @pl.when((qi == 0) & (ki == 0))
    def _():
        kj = lax.broadcasted_iota(jnp.int32, (tq, tq), 0)
        qq = lax.broadcasted_iota(jnp.int32, (tq, tq), 1)
        far = rb_ref[h * NUM_BUCKETS + NUM_BUCKETS - 1]
        for which in range(2):
            dist = qq - kj + which * tq
            bucket = _t5_bucket(dist)
            val = jnp.zeros((tq, tq), F32)
            for b in range(NUM_BUCKETS - 2, -1, -1):
                val = jnp.where(bucket == b, (rb_ref[h * NUM_BUCKETS + b] - far) * LOG2E, val)
            if which == 0:
                val = jnp.where(dist >= 0, val, NEG)
            tab[which] = val

    @pl.when(ki == 0)
    def _():
        q = q_ref[...] * ((A_DQK ** -0.5) * LOG2E)
        lane = lax.broadcasted_iota(jnp.int32, (tq, 2 * A_DQK), 1)
        qs[0:tq, :] = jnp.where(lane < A_DQK, q, 0.0).astype(BF16)
        qs[tq:2 * tq, :] = jnp.where(lane >= A_DQK, q, 0.0).astype(BF16)
        m_scr[...] = jnp.full(m_scr.shape, NEG, F32)
        l_scr[...] = jnp.zeros(l_scr.shape, F32)
        acc[...] = jnp.zeros(acc.shape, F32)

    def step(which):
        k16 = k_ref[...].astype(BF16)
        vt16 = vt_ref[...]
        n_chunks = 2 * tq // cw
        s_next = _dot_nt(k16, qs[0:cw, :])
        for c in range(n_chunks):
            cols = slice(c * cw, (c + 1) * cw)
            s = s_next
            if c + 1 < n_chunks:
                s_next = _dot_nt(k16, qs[(c + 1) * cw:(c + 2) * cw, :])
            if which is not None:
                q0 = (c * cw) % tq
                s = s + tab[which, :, q0:q0 + cw]
            m_prev = m_scr[0:1, cols]
            m_new = jnp.maximum(m_prev, jnp.max(s, axis=0, keepdims=True))
            alpha = jnp.exp2(m_prev - m_new)
            p = jnp.exp2(s - m_new)
            l_scr[0:1, cols] = alpha * l_scr[0:1, cols] + jnp.sum(p, axis=0, keepdims=True)
            acc[:, cols] = alpha * acc[:, cols] + _dot(vt16, p.astype(BF16))
            m_scr[0:1, cols] = m_new

    @pl.when(ki < qi - 1)
    def _():
        step(None)

    @pl.when(ki == qi - 1)
    def _():
        step(1)

    @pl.when(ki == qi)
    def _():
        step(0)
        o = acc[...] / l_scr[0:1, :]
        lam = _diff_lambda(lp_ref[...], lam_init)
        a = o[:, 0:tq] - lam * o[:, tq:2 * tq]
        a = a * lax.rsqrt(jnp.mean(a * a, axis=0, keepdims=True) + EPS)
        o_ref[...] = (a.T * g_ref[...] * (1.0 - lam_init)).astype(o_ref.dtype)


def _attn_prompt(u_all, nb, t_len, rel_bias_flat, a_lambda, subln_g, layer, name):
    tq = _tile(t_len, 512, LANE)
    nq = t_len // tq
    lam_init = _lambda_init(layer)
    mp = nb * t_len
    vt = u_all[:mp, C_AV:C_AV + A_W].astype(BF16).reshape(nb, t_len, A_HEADS, A_DV)
    vt = vt.transpose(0, 2, 3, 1).reshape(nb * A_HEADS * A_DV, t_len)
    pairs = [(qi, ki) for qi in range(nq) for ki in range(qi + 1)]
    qi_of = jnp.array([p[0] for p in pairs], jnp.int32)
    ki_of = jnp.array([p[1] for p in pairs], jnp.int32)
    grid_spec = pltpu.PrefetchScalarGridSpec(
        num_scalar_prefetch=2,
        grid=(nb, A_HEADS, len(pairs)),
        in_specs=[
            pl.BlockSpec(memory_space=pltpu.SMEM),
            pl.BlockSpec((tq, A_DV), lambda b, h, p, qi, ki: (b * nq + qi[p], C_AQ // A_DV + h)),
            pl.BlockSpec((tq, A_DV), lambda b, h, p, qi, ki: (b * nq + ki[p], C_AK // A_DV + h)),
            pl.BlockSpec((A_DV, tq), lambda b, h, p, qi, ki: (b * A_HEADS + h, ki[p])),
            pl.BlockSpec((4, A_DQK), lambda b, h, p, qi, ki: (0, 0)),
            pl.BlockSpec((1, A_DV), lambda b, h, p, qi, ki: (0, 0)),
        ],
        out_specs=pl.BlockSpec((tq, A_DV), lambda b, h, p, qi, ki: (b * nq + qi[p], h)),
        scratch_shapes=[pltpu.VMEM((2 * tq, A_DV), BF16),
                        pltpu.VMEM((SUBLANE, 2 * tq), F32),
                        pltpu.VMEM((SUBLANE, 2 * tq), F32),
                        pltpu.VMEM((A_DV, 2 * tq), F32),
                        pltpu.VMEM((2, tq, tq), F32)],
    )
    return pl.pallas_call(
        functools.partial(_attn_prompt_kernel, tq=tq, lam_init=lam_init),
        out_shape=jax.ShapeDtypeStruct((mp, A_W), BF16),
        grid_spec=grid_spec,
        compiler_params=_params("arbitrary", "arbitrary", "arbitrary"),
        name=name,
    )(qi_of, ki_of, rel_bias_flat, u_all, u_all, vt, a_lambda, subln_g)


def _attn_sample_kernel(pt_ref, rb_ref, q_ref, kn_ref, vn_ref, lp_ref, g_ref, *rest,
                        n_pg, t_len, page, past, lam_init):
    kp = rest[0:n_pg]
    vp = rest[n_pg:2 * n_pg]
    o_ref = rest[2 * n_pg]
    qs, m_scr, l_scr, acc, knp, vnp = rest[2 * n_pg + 1:]
    s_id = pl.program_id(1)
    n_steps = pl.num_programs(1)
    n_cache = n_steps - 1
    rows = A_HEADS * 2 * t_len
    cols = page * A_HEADS
    row = lax.broadcasted_iota(jnp.int32, (rows, 1), 0)
    row_h = row // (2 * t_len)
    row_q = row % t_len

    def head_col(b):
        c = jnp.full((rows, 1), rb_ref[b], F32)
        for hh in range(1, A_HEADS):
            c = jnp.where(row_h == hh, rb_ref[hh * NUM_BUCKETS + b], c)
        return c

    def exact_bias(dist):
        bucket = _t5_bucket(dist)
        far = head_col(NUM_BUCKETS - 1)
        val = jnp.zeros(dist.shape, F32)
        for b in range(NUM_BUCKETS - 2, -1, -1):
            val = jnp.where(bucket == b, (head_col(b) - far) * LOG2E, val)
        return val

    @pl.when(s_id == 0)
    def _():
        q = q_ref[...] * ((A_DQK ** -0.5) * LOG2E)
        lane = lax.broadcasted_iota(jnp.int32, (t_len, A_DV), 1)
        for hh in range(A_HEADS):
            qh = q[:, hh * A_DV:(hh + 1) * A_DV]
            qs[hh * 2 * t_len:hh * 2 * t_len + t_len, :] = jnp.where(lane < A_DQK, qh, 0.0).astype(BF16)
            qs[hh * 2 * t_len + t_len:(hh + 1) * 2 * t_len, :] = jnp.where(lane >= A_DQK, qh, 0.0).astype(BF16)
        m_scr[...] = jnp.full(m_scr.shape, NEG, F32)
        l_scr[...] = jnp.zeros(l_scr.shape, F32)
        acc[...] = jnp.zeros(acc.shape, F32)

    def head_rows(refs, hh):
        return jnp.concatenate([r[pl.ds(hh, page, stride=A_HEADS), :].astype(BF16) for r in refs], axis=0)

    def cache_step(exact):
        n_keys = n_pg * page
        s_all = jnp.concatenate(
            [_dot_nt(qs[hh * 2 * t_len:(hh + 1) * 2 * t_len, :], head_rows(kp, hh)) for hh in range(A_HEADS)],
            axis=0)
        if exact:
            k_pos = (s_id - 1) * n_keys + lax.broadcasted_iota(jnp.int32, (rows, n_keys), 1)
            s_all = s_all + exact_bias(past + row_q - k_pos)

        def pv(p):
            return jnp.concatenate(
                [_dot(p[hh * 2 * t_len:(hh + 1) * 2 * t_len, :].astype(BF16), head_rows(vp, hh))
                 for hh in range(A_HEADS)], axis=0)

        _softmax_step(s_all, m_scr, l_scr, acc, pv)

    @pl.when((s_id >= 1) & (s_id < n_cache))
    def _():
        cache_step(False)

    @pl.when(s_id == n_cache)
    def _():
        cache_step(True)

    @pl.when(s_id == 0)
    def _():
        knp[...] = jnp.zeros(knp.shape, BF16)
        vnp[...] = jnp.zeros(vnp.shape, BF16)
        kn = kn_ref[...]
        vn = vn_ref[...]
        for hh in range(A_HEADS):
            knp[hh, 0:t_len, :] = kn[:, hh * A_DV:(hh + 1) * A_DV].astype(BF16)
            vnp[hh, 0:t_len, :] = vn[:, hh * A_DV:(hh + 1) * A_DV].astype(BF16)
        kcol = lax.broadcasted_iota(jnp.int32, (rows, page), 1)
        dist = row_q - kcol
        bias = jnp.where((dist >= 0) & (kcol < t_len), exact_bias(dist), NEG)
        parts = []
        for hh in range(A_HEADS):
            parts.append(_dot_nt(qs[hh * 2 * t_len:(hh + 1) * 2 * t_len, :], knp[hh]))
        s = jnp.concatenate(parts, axis=0) + bias

        def pv(p):
            outs = []
            for hh in range(A_HEADS):
                outs.append(_dot(p[hh * 2 * t_len:(hh + 1) * 2 * t_len, :].astype(BF16), vnp[hh]))
            return jnp.concatenate(outs, axis=0)

        _softmax_step(s, m_scr, l_scr, acc, pv)

    @pl.when(s_id == n_cache)
    def _():
        o = acc[...] / l_scr[...]
        lam = _diff_lambda(lp_ref[...], lam_init)
        g = g_ref[...]
        for hh in range(A_HEADS):
            r0 = hh * 2 * t_len
            a = o[r0:r0 + t_len, :] - lam * o[r0 + t_len:r0 + 2 * t_len, :]
            o_ref[:, hh * A_DV:(hh + 1) * A_DV] = (_rms(a, g) * (1.0 - lam_init)).astype(o_ref.dtype)


def _attn_sample(u_all, row0, nb, t_len, cache_k, cache_v, page_table, rel_bias_flat, a_lambda,
                 subln_g, layer, name):
    depth, n_pool, page = cache_k.shape[0], cache_k.shape[1], cache_k.shape[2]
    n_pages = page_table.shape[1]
    n_pg = _tile(n_pages, 16, 1)
    n_cache = n_pages // n_pg
    past = n_pages * page
    rb0 = row0 // t_len
    rows = A_HEADS * 2 * t_len
    ck = cache_k.reshape(depth, n_pool, page * A_HEADS, A_DV)
    cv = cache_v.reshape(depth, n_pool, page * A_HEADS, A_DV)
    pt = page_table.reshape(-1)

    def page_spec(i):
        def imap(b, s, pt_ref):
            step = jnp.maximum(s - 1, 0)
            return (layer, pt_ref[b * n_pages + step * n_pg + i], 0, 0)
        return pl.BlockSpec((None, None, page * A_HEADS, A_DV), imap)

    grid_spec = pltpu.PrefetchScalarGridSpec(
        num_scalar_prefetch=1,
        grid=(nb, n_cache + 1),
        in_specs=[
            pl.BlockSpec(memory_space=pltpu.SMEM),
            pl.BlockSpec((t_len, A_W), lambda b, s, pt_ref: (rb0 + b, C_AQ // A_W)),
            pl.BlockSpec((t_len, A_W), lambda b, s, pt_ref: (rb0 + b, C_AK // A_W)),
            pl.BlockSpec((t_len, A_W), lambda b, s, pt_ref: (rb0 + b, C_AV // A_W)),
            pl.BlockSpec((4, A_DQK), lambda b, s, pt_ref: (0, 0)),
            pl.BlockSpec((1, A_DV), lambda b, s, pt_ref: (0, 0)),
        ] + [page_spec(i) for i in range(n_pg)] + [page_spec(i) for i in range(n_pg)],
        out_specs=pl.BlockSpec((t_len, A_W), lambda b, s, pt_ref: (b, 0)),
        scratch_shapes=[pltpu.VMEM((rows, A_DV), BF16),
                        pltpu.VMEM((rows, 1), F32),
                        pltpu.VMEM((rows, 1), F32),
                        pltpu.VMEM((rows, A_DV), F32),
                        pltpu.VMEM((A_HEADS, page, A_DV), BF16),
                        pltpu.VMEM((A_HEADS, page, A_DV), BF16)],
    )
    return pl.pallas_call(
        functools.partial(_attn_sample_kernel, n_pg=n_pg, t_len=t_len, page=page, past=past,
                          lam_init=_lambda_init(layer)),
        out_shape=jax.ShapeDtypeStruct((nb * t_len, A_W), BF16),
        grid_spec=grid_spec,
        compiler_params=_params("arbitrary", "arbitrary"),
        name=name,
    )(pt, rel_bias_flat, u_all, u_all, u_all, a_lambda, subln_g, *([ck] * n_pg), *([cv] * n_pg))


def _merge_kernel(x_ref, og_ref, oa_ref, ol_ref, gt_ref, wg_ref, wa_ref, wl_ref, wo_ref, o_ref):
    d = x_ref.shape[1]
    gt = gt_ref[...].astype(F32)
    m = gt[:, 0:d] * _dot(og_ref[...], wg_ref[...])
    m = m + gt[:, d:2 * d] * _dot(oa_ref[...], wa_ref[...])
    m = m + gt[:, 2 * d:3 * d] * _dot(ol_ref[...], wl_ref[...])
    o_ref[...] = x_ref[...] + _dot(m.astype(BF16), wo_ref[...])


def _merge(x, o_g, o_a, o_l, gates, wg, wa, wl, wo, name):
    m, d = x.shape
    tm = _tile(m, 640)
    row = lambda w: pl.BlockSpec((tm, w), lambda i: (i, 0))
    full = lambda a: pl.BlockSpec(a.shape, lambda i: (0, 0))
    return pl.pallas_call(
        _merge_kernel,
        out_shape=jax.ShapeDtypeStruct((m, d), F32),
        grid=(m // tm,),
        in_specs=[row(d), row(G_W), row(A_W), row(L_WIDTH), row(3 * d),
                  full(wg), full(wa), full(wl), full(wo)],
        out_specs=row(d),
        compiler_params=_params("parallel"),
        name=name,
    )(x, o_g, o_a, o_l, gates, wg, wa, wl, wo)


def _ffn_kernel(x_ref, g_ref, w1_ref, w3_ref, w2_ref, o_ref, h_scr, acc):
    f = pl.program_id(1)

    @pl.when(f == 0)
    def _():
        x = x_ref[...]
        h_scr[...] = _rms(x, g_ref[...]).astype(BF16)
        acc[...] = x

    h = h_scr[...]
    t = _silu(_dot(h, w1_ref[...])) * _dot(h, w3_ref[...])
    acc[...] += _dot(t.astype(BF16), w2_ref[...])

    @pl.when(f == pl.num_programs(1) - 1)
    def _():
        o_ref[...] = acc[...]


def _ffn(x, g, w1, w3, w2, name):
    m, d = x.shape
    ff = w1.shape[1]
    tm = _tile(m, 1280)
    tf = _tile(ff, 512, LANE)
    return pl.pallas_call(
        _ffn_kernel,
        out_shape=jax.ShapeDtypeStruct((m, d), F32),
        grid=(m // tm, ff // tf),
        in_specs=[
            pl.BlockSpec((tm, d), lambda i, f: (i, 0)),
            pl.BlockSpec((1, d), lambda i, f: (0, 0)),
            pl.BlockSpec((d, tf), lambda i, f: (0, f)),
            pl.BlockSpec((d, tf), lambda i, f: (0, f)),
            pl.BlockSpec((tf, d), lambda i, f: (f, 0)),
        ],
        out_specs=pl.BlockSpec((tm, d), lambda i, f: (i, 0)),
        scratch_shapes=[pltpu.VMEM((tm, d), BF16), pltpu.VMEM((tm, d), F32)],
        compiler_params=_params("parallel", "arbitrary"),
        name=name,
    )(x, g, w1, w3, w2)


def _router_kernel(x_ref, g_ref, wr_ref, h_ref, o_ref):
    h = _rms(x_ref[...], g_ref[...])
    h_ref[...] = h.astype(BF16)
    wr_hi, wr_lo = _split(wr_ref[...])
    logits = _dot_split(h, wr_hi, wr_lo)
    lane = lax.broadcasted_iota(jnp.int32, logits.shape, 1)
    logits = jnp.where(lane < N_EXPERTS, logits, -jnp.inf)
    m1 = jnp.max(logits, axis=-1, keepdims=True)
    i1 = jnp.min(jnp.where(logits == m1, lane, LANE), axis=-1, keepdims=True)
    rest = jnp.where(lane == i1, -jnp.inf, logits)
    m2 = jnp.max(rest, axis=-1, keepdims=True)
    i2 = jnp.min(jnp.where(rest == m2, lane, LANE), axis=-1, keepdims=True)
    e2 = jnp.exp(m2 - m1)
    den = 1.0 + e2
    o_ref[...] = jnp.where(lane == 0, i1.astype(F32),
                           jnp.where(lane == 1, i2.astype(F32),
                                     jnp.where(lane == 2, 1.0 / den, jnp.where(lane == 3, e2 / den, 0.0))))


def _router(x, g, wr, name):
    m, d = x.shape
    tm = _tile(m, 1280)
    return pl.pallas_call(
        _router_kernel,
        out_shape=(jax.ShapeDtypeStruct((m, d), BF16), jax.ShapeDtypeStruct((m, LANE), F32)),
        grid=(m // tm,),
        in_specs=[pl.BlockSpec((tm, d), lambda i: (i, 0)),
                  pl.BlockSpec((1, d), lambda i: (0, 0)),
                  pl.BlockSpec((d, LANE), lambda i: (0, 0))],
        out_specs=(pl.BlockSpec((tm, d), lambda i: (i, 0)), pl.BlockSpec((tm, LANE), lambda i: (i, 0))),
        compiler_params=_params("parallel"),
        name=name,
    )(x, g, wr)


MOE_ROW_BLOCK = 640


def _expert_ffn_kernel(be_ref, na_ref, x_ref, w1_ref, w3_ref, w2_ref, o_ref, acc):
    b = pl.program_id(0)
    f = pl.program_id(1)
    active = b < na_ref[0]

    def partial_out():
        x = x_ref[...]
        t = _silu(_dot(x, w1_ref[...])) * _dot(x, w3_ref[...])
        return _dot(t.astype(BF16), w2_ref[...])

    @pl.when(active & (f == 0))
    def _():
        acc[...] = partial_out()

    @pl.when(active & (f > 0))
    def _():
        acc[...] += partial_out()

    @pl.when(jnp.logical_not(active) & (f == 0))
    def _():
        acc[...] = jnp.zeros(acc.shape, F32)

    @pl.when(f == pl.num_programs(1) - 1)
    def _():
        o_ref[...] = acc[...]


def _expert_ffn(xs, block_expert, n_active, w1, w3, w2, name):
    n, d = xs.shape
    ff = w1.shape[2]
    rb = MOE_ROW_BLOCK
    tf = _tile(ff, 512, LANE)
    grid_spec = pltpu.PrefetchScalarGridSpec(
        num_scalar_prefetch=2,
        grid=(n // rb, ff // tf),
        in_specs=[
            pl.BlockSpec((rb, d), lambda b, f, be, na: (b, 0)),
            pl.BlockSpec((None, d, tf), lambda b, f, be, na: (be[b], 0, f)),
            pl.BlockSpec((None, d, tf), lambda b, f, be, na: (be[b], 0, f)),
            pl.BlockSpec((None, tf, d), lambda b, f, be, na: (be[b], f, 0)),
        ],
        out_specs=pl.BlockSpec((rb, d), lambda b, f, be, na: (b, 0)),
        scratch_shapes=[pltpu.VMEM((rb, d), F32)],
    )
    return pl.pallas_call(
        _expert_ffn_kernel,
        out_shape=jax.ShapeDtypeStruct((n, d), F32),
        grid_spec=grid_spec,
        compiler_params=_params("arbitrary", "arbitrary"),
        name=name,
    )(block_expert, n_active, xs, w1, w3, w2)


def _combine_kernel(x_ref, y1_ref, y2_ref, r_ref, fg_ref, o_ref):
    r = r_ref[...]
    lane = lax.broadcasted_iota(jnp.int32, r.shape, 1)
    w1 = jnp.sum(jnp.where(lane == 2, r, 0.0), axis=-1, keepdims=True)
    w2 = jnp.sum(jnp.where(lane == 3, r, 0.0), axis=-1, keepdims=True)
    o_ref[...] = _rms(x_ref[...] + (w1 * y1_ref[...] + w2 * y2_ref[...]), fg_ref[...])


def _combine(x, y1, y2, route, final_g, name):
    m, d = x.shape
    tm = _tile(m, 640)
    row = lambda w: pl.BlockSpec((tm, w), lambda i: (i, 0))
    return pl.pallas_call(
        _combine_kernel,
        out_shape=jax.ShapeDtypeStruct((m, d), F32),
        grid=(m // tm,),
        in_specs=[row(d), row(d), row(d), row(LANE), pl.BlockSpec((1, d), lambda i: (0, 0))],
        out_specs=row(d),
        compiler_params=_params("parallel"),
        name=name,
    )(x, y1, y2, route, final_g)


def _moe_routed(x, g, wr, w1, w3, w2, final_g, layer):
    m, d = x.shape
    ne = w1.shape[0]
    rb = MOE_ROW_BLOCK
    h16, route = _router(x, g, wr, f"router_{layer}")
    eid = route[:, 0:TOP_K].astype(jnp.int32).reshape(-1)
    n_assign = TOP_K * m
    order = jnp.argsort(eid, stable=True)
    counts = jnp.sum(eid[:, None] == jnp.arange(ne, dtype=jnp.int32)[None, :], axis=0, dtype=jnp.int32)
    padded = (counts + rb - 1) // rb * rb
    pad_end = jnp.cumsum(padded)
    sorted_e = eid[order]
    rank = jnp.arange(n_assign, dtype=jnp.int32) - (jnp.cumsum(counts) - counts)[sorted_e]
    dest = (pad_end - padded)[sorted_e] + rank
    n_rows = (n_assign + rb - 1) // rb * rb + ne * rb
    src_tok = jnp.zeros((n_rows,), jnp.int32).at[dest].set(order // TOP_K)
    pos = jnp.zeros((n_assign,), jnp.int32).at[order].set(dest).reshape(m, TOP_K)
    block_start = jnp.arange(n_rows // rb, dtype=jnp.int32) * rb
    block_expert = jnp.minimum(jnp.searchsorted(pad_end, block_start, side="right"), ne - 1).astype(jnp.int32)
    n_active = (pad_end[ne - 1:ne] // rb).astype(jnp.int32)
    xs = jnp.take(h16, src_tok, axis=0)
    ys = _expert_ffn(xs, block_expert, n_active, w1, w3, w2, f"moe_{layer}")
    y1 = jnp.take(ys, pos[:, 0], axis=0)
    y2 = jnp.take(ys, pos[:, 1], axis=0)
    return _combine(x, y1, y2, route, final_g, f"moe_combine_{layer}")


def _final_norm_kernel(x_ref, g_ref, o_ref):
    o_ref[...] = _rms(x_ref[...], g_ref[...])


def _final_norm(x, g, name):
    m, d = x.shape
    tm = _tile(m, 1280)
    return pl.pallas_call(
        _final_norm_kernel,
        out_shape=jax.ShapeDtypeStruct((m, d), F32),
        grid=(m // tm,),
        in_specs=[pl.BlockSpec((tm, d), lambda i: (i, 0)), pl.BlockSpec((1, d), lambda i: (0, 0))],
        out_specs=pl.BlockSpec((tm, d), lambda i: (i, 0)),
        compiler_params=_params("parallel"),
        name=name,
    )(x, g)


def _split_w_in(w):
    o_gb = 4 * G_W
    o_aq = o_gb + 2 * G_HEADS
    pad = jnp.zeros((w.shape[0], LANE - 2 * G_HEADS), w.dtype)
    return (jnp.concatenate([w[:, :o_gb], w[:, o_aq:]], axis=1),
            jnp.concatenate([w[:, o_gb:o_aq], pad], axis=1))


def _block_diag(w):
    n, d, _ = w.shape
    eye = jnp.eye(n, dtype=w.dtype)
    return (eye[:, None, :, None] * w[:, :, None, :]).reshape(n * d, n * d)


def kernel(x_prompt, x_sample, cache_k, cache_v, page_table, state_gdn, state_gdn_conv, state_lru, state_lru_conv, norm1_g, norm2_g, final_g, w_in, w_gate, b_gate, g_conv_w, g_a_log, g_dt_bias, g_norm_g, a_lambda, a_subln_g, rel_bias, l_conv_w, l_conv_b, l_wa, l_ba, l_wx, l_bx, l_lambda, w_br_g, w_br_a, w_br_l, w_out, f_w1, f_w3, f_w2, m_router, m_w1, m_w3, m_w2):
    bp, tp, d = x_prompt.shape
    bs, ts, _ = x_sample.shape
    depth = w_in.shape[0]
    mp = bp * tp
    ms = bs * ts
    x = jnp.concatenate([x_prompt.reshape(mp, d), x_sample.reshape(ms, d)], axis=0)
    rb_flat = rel_bias.T.reshape(-1)
    row = lambda v: v.reshape(1, -1)

    k_p, v_p, k_s, v_s = [], [], [], []
    sg_p, sg_s, cg_p, cg_s, sl_p, sl_s, cl_p, cl_s = [], [], [], [], [], [], [], []
    y = None
    for l in range(depth):
        w_main, w_gba = _split_w_in(w_in[l])
        u = _norm_matmul(x, row(norm1_g[l]), w_main.astype(BF16), jnp.zeros((1, U_COLS), F32),
                         None, F32, f"in_proj_{l}")
        gba = _norm_matmul(x, row(norm1_g[l]), w_gba.astype(BF16), jnp.zeros((1, LANE), F32),
                           None, F32, f"in_proj_gba_{l}")
        gates = _norm_matmul(x, row(norm1_g[l]), w_gate[l].astype(BF16), row(b_gate[l]),
                             "sigmoid", BF16, f"gate_proj_{l}")

        ab = jnp.zeros((2, LANE), F32)
        ab = ab.at[0, G_HEADS:2 * G_HEADS].set(g_a_log[l]).at[1, G_HEADS:2 * G_HEADS].set(g_dt_bias[l])
        ng = row(g_norm_g[l])
        og_p, s_p = _gdn(u, gba, 0, bp, tp, jnp.zeros((bp, CONV_W - 1, 3 * G_W), F32),
                         jnp.zeros((bp, G_HEADS, G_DK, G_DV), F32), g_conv_w[l], ab, ng, f"gdn_prompt_{l}")
        og_s, s_s = _gdn(u, gba, mp, bs, ts, state_gdn_conv[l], state_gdn[l], g_conv_w[l], ab, ng,
                         f"gdn_sample_{l}")

        sub_g = row(a_subln_g[l])
        oa_p = _attn_prompt(u, bp, tp, rb_flat, a_lambda[l], sub_g, l, f"attn_prompt_{l}")
        oa_s = _attn_sample(u, mp, bs, ts, cache_k, cache_v, page_table, rb_flat, a_lambda[l], sub_g, l,
                            f"attn_sample_{l}")

        wa_bd = _block_diag(l_wa[l]).astype(BF16)
        wx_bd = _block_diag(l_wx[l]).astype(BF16)
        lru_args = (l_conv_w[l], row(l_conv_b[l]), wa_bd, row(l_ba[l]), wx_bd, row(l_bx[l]), row(l_lambda[l]))
        ol_p, h_p = _lru(u, 0, bp, tp, jnp.zeros((bp, CONV_W - 1, L_WIDTH), F32),
                         jnp.zeros((bp, L_WIDTH), F32), *lru_args, f"lru_prompt_{l}")
        ol_s, h_s = _lru(u, mp, bs, ts, state_lru_conv[l], state_lru[l], *lru_args, f"lru_sample_{l}")

        x = _merge(x, jnp.concatenate([og_p, og_s]), jnp.concatenate([oa_p, oa_s]),
                   jnp.concatenate([ol_p, ol_s]), gates, w_br_g[l].astype(BF16), w_br_a[l].astype(BF16),
                   w_br_l[l].astype(BF16), w_out[l].astype(BF16), f"merge_{l}")

        j = l // 2
        if l % 2 == 0:
            x = _ffn(x, row(norm2_g[l]), f_w1[j].astype(BF16), f_w3[j].astype(BF16), f_w2[j].astype(BF16),
                     f"ffn_{l}")
            if l == depth - 1:
                y = _final_norm(x, row(final_g), "final_norm")
        else:
            wr = jnp.zeros((d, LANE), F32).at[:, :N_EXPERTS].set(m_router[j])
            if l != depth - 1:
                raise NotImplementedError("mixture layer is only fused with the final norm")
            y = _moe_routed(x, row(norm2_g[l]), wr, m_w1[j].astype(BF16), m_w3[j].astype(BF16),
                            m_w2[j].astype(BF16), row(final_g), l)

        def tail_rows(c0, width):
            keep = CONV_W - 1
            p = jnp.stack([u[(b + 1) * tp - keep:(b + 1) * tp, c0:c0 + width] for b in range(bp)])
            s = u[mp:, c0:c0 + width].reshape(bs, ts, width)[:, ts - keep:]
            return p, s

        k_p.append(u[:mp, C_AK:C_AK + A_W].reshape(bp, tp, A_HEADS, A_DV))
        v_p.append(u[:mp, C_AV:C_AV + A_W].reshape(bp, tp, A_HEADS, A_DV))
        k_s.append(u[mp:, C_AK:C_AK + A_W].reshape(bs, ts, A_HEADS, A_DV))
        v_s.append(u[mp:, C_AV:C_AV + A_W].reshape(bs, ts, A_HEADS, A_DV))
        sg_p.append(s_p)
        sg_s.append(s_s)
        cp, cs = tail_rows(C_GQKV, 3 * G_W)
        cg_p.append(cp)
        cg_s.append(cs)
        sl_p.append(h_p)
        sl_s.append(h_s)
        cp, cs = tail_rows(C_LX, L_WIDTH)
        cl_p.append(cp)
        cl_s.append(cs)

    st = jnp.stack
    return (y[:mp].reshape(bp, tp, d), y[mp:].reshape(bs, ts, d),
            st(k_p), st(v_p), st(k_s), st(v_s), st(sg_p), st(sg_s), st(cg_p), st(cg_s),
            st(sl_p), st(sl_s), st(cl_p), st(cl_s))
```

```python
import functools
import math

import jax
import jax.numpy as jnp
from jax import lax
from jax.experimental import pallas as pl
from jax.experimental.pallas import tpu as pltpu

F32 = jnp.float32
BF16 = jnp.bfloat16
EPS = 1e-6
NEG = -1e30

G_HEADS, G_DK, G_DV = 4, 128, 128
G_W = G_HEADS * G_DK
GDN_CHUNK = 64
GDN_SEQS_PER_STEP = 2
CONV_W = 4
A_HEADS, A_DQK, A_DV = 4, 64, 128
A_W = A_HEADS * A_DV
L_WIDTH, L_BLOCKS = 512, 8
LRU_C = 8.0
NUM_BUCKETS, MAX_DISTANCE = 32, 128
N_EXPERTS, TOP_K = 8, 2
LANE = 128
SUBLANE = 8
VMEM_LIMIT = 56 * 1024 * 1024

C_GQKV = 0
C_GZ = 3 * G_W
C_AQ = 4 * G_W
C_AK = C_AQ + A_W
C_AV = C_AK + A_W
C_LX = C_AV + A_W
C_LY = C_LX + L_WIDTH
U_COLS = C_LY + L_WIDTH


def _tile(n, target, mult=SUBLANE):
    best = None
    for d in range(mult, min(n, target) + 1, mult):
        if n % d == 0:
            best = d
    return best if best is not None else n


def _params(*sem):
    return pltpu.CompilerParams(dimension_semantics=sem, vmem_limit_bytes=VMEM_LIMIT)


def _dot(a, b):
    return jnp.dot(a, b, preferred_element_type=F32)


def _dot_nt(a, b):
    return lax.dot_general(a, b, (((1,), (1,)), ((), ())), preferred_element_type=F32)


def _dot_tn(a, b):
    return lax.dot_general(a, b, (((0,), (0,)), ((), ())), preferred_element_type=F32)


def _split(x):
    hi = x.astype(BF16)
    return hi, (x - hi.astype(F32)).astype(BF16)


def _dot_split(a, b_hi, b_lo):
    a_hi, a_lo = _split(a)
    return _dot(a_hi, b_hi) + (_dot(a_hi, b_lo) + _dot(a_lo, b_hi))


def _dot_lhs_exact(a16, b):
    b_hi, b_lo = _split(b)
    return _dot(a16, b_hi) + _dot(a16, b_lo)


def _sigmoid(x):
    return 1.0 / (1.0 + jnp.exp(-x))


def _silu(x):
    return x * _sigmoid(x)


def _softplus(x):
    return jnp.maximum(x, 0.0) + jnp.log1p(jnp.exp(-jnp.abs(x)))


def _rms(x, g):
    return x * lax.rsqrt(jnp.mean(x * x, axis=-1, keepdims=True) + EPS) * g


def _norm_mm_kernel(x_ref, g_ref, w_ref, b_ref, o_ref, h_scr, *, act):
    @pl.when(pl.program_id(1) == 0)
    def _():
        h_scr[...] = _rms(x_ref[...], g_ref[...]).astype(BF16)

    acc = _dot(h_scr[...], w_ref[...]) + b_ref[...]
    if act == "sigmoid":
        acc = _sigmoid(acc)
    o_ref[...] = acc.astype(o_ref.dtype)


def _norm_matmul(x, g, w, b, act, out_dtype, name):
    m, k = x.shape
    n = w.shape[1]
    tm = _tile(m, 1280)
    tn = _tile(n, 512, LANE)
    return pl.pallas_call(
        functools.partial(_norm_mm_kernel, act=act),
        out_shape=jax.ShapeDtypeStruct((m, n), out_dtype),
        grid=(m // tm, n // tn),
        in_specs=[
            pl.BlockSpec((tm, k), lambda i, j: (i, 0)),
            pl.BlockSpec((1, k), lambda i, j: (0, 0)),
            pl.BlockSpec((k, tn), lambda i, j: (0, j)),
            pl.BlockSpec((1, tn), lambda i, j: (0, j)),
        ],
        out_specs=pl.BlockSpec((tm, tn), lambda i, j: (i, j)),
        scratch_shapes=[pltpu.VMEM((tm, k), BF16)],
        compiler_params=_params("parallel", "arbitrary"),
        name=name,
    )(x, g, w, b)


def _gdn_kernel(*refs, ns, tt, chunk, rows):
    seq_in = [refs[3 * i:3 * i + 3] for i in range(ns)]
    cs_ref, s0_ref, cw_ref, ab_ref, ng_ref, o_ref, sout_ref, xe, ys, bg, s_scr = refs[3 * ns:]
    t = pl.program_id(1)
    nt = pl.num_programs(1)
    hc = G_HEADS * chunk

    @pl.when(t == 0)
    def _():
        for s in range(ns):
            for h in range(G_HEADS):
                s_scr[s, :, h * G_DV:(h + 1) * G_DV] = s0_ref[s, h]
            xe[s] = jnp.zeros(xe.shape[1:], F32)
            xe[s, SUBLANE - (CONV_W - 1):SUBLANE, :] = cs_ref[s]

    cw = cw_ref[...]
    ab = ab_ref[...]
    base = SUBLANE - (CONV_W - 1)
    for s in range(ns):
        qkv_ref, _, gba_ref = seq_in[s]
        xe[s, SUBLANE:SUBLANE + tt, :] = qkv_ref[...]
        y = xe[s, base:base + rows, :] * cw[0:1, :]
        for j in range(1, CONV_W):
            y = y + xe[s, base + j:base + j + rows, :] * cw[j:j + 1, :]
        xe[s, 0:SUBLANE, :] = xe[s, tt:tt + SUBLANE, :]

        gba = gba_ref[...]
        if rows > tt:
            gba = jnp.concatenate([gba, jnp.zeros((rows - tt, LANE), F32)], axis=0)
        lane_g = lax.broadcasted_iota(jnp.int32, (rows, LANE), 1)
        bgc = jnp.where(lane_g < G_HEADS, _sigmoid(gba), -jnp.exp(ab[0:1, :]) * _softplus(gba + ab[1:2, :]))
        y = _silu(y)
        if rows > tt:
            live = lax.broadcasted_iota(jnp.int32, (rows, 1), 0) < tt
            y = jnp.where(live, y, 0.0)
            bgc = jnp.where(live, bgc, 0.0)
        bg[s] = bgc
        for h in range(G_HEADS):
            q = y[:, h * G_DK:(h + 1) * G_DK]
            k = y[:, G_W + h * G_DK:G_W + (h + 1) * G_DK]
            ys[s, :, h * G_DK:(h + 1) * G_DK] = (
                q * lax.rsqrt(jnp.sum(q * q, axis=-1, keepdims=True) + 1e-6) * (G_DK ** -0.5))
            ys[s, :, G_W + h * G_DK:G_W + (h + 1) * G_DK] = (
                k * lax.rsqrt(jnp.sum(k * k, axis=-1, keepdims=True) + 1e-6))
        ys[s, :, 2 * G_W:3 * G_W] = y[:, 2 * G_W:3 * G_W]

    def iota(shape, d):
        return lax.broadcasted_iota(jnp.int32, shape, d)

    r_c = iota((chunk, hc), 0)
    j_c = iota((chunk, hc), 1) % chunk
    tril_cat = r_c >= j_c
    strict_cat = r_c > j_c
    eye_cat = (r_c == j_c).astype(F32)
    triu_cat = (r_c <= j_c).astype(F32)
    tril_f = jnp.where(iota((chunk, chunk), 0) >= iota((chunk, chunk), 1), 1.0, 0.0).astype(BF16)
    ones_cc = jnp.ones((chunk, chunk), BF16)
    l_s = iota((LANE, hc + 2 * G_W), 0)
    n_s = iota((LANE, hc + 2 * G_W), 1)
    src_lane = jnp.where(n_s < hc, G_HEADS + n_s // chunk,
                         jnp.where(n_s < hc + G_W, G_HEADS + (n_s - hc) // G_DK,
                                   (n_s - hc - G_W) // G_DK))
    sel = jnp.where(l_s == src_lane, 1.0, 0.0).astype(BF16)
    bd_cc = iota((hc, hc), 0) // chunk == iota((hc, hc), 1) // chunk
    bd_cd = iota((hc, G_W), 0) // chunk == iota((hc, G_W), 1) // G_DK
    bd_dd = iota((G_W, G_W), 0) // G_DK == iota((G_W, G_W), 1) // G_DK
    ng = ng_ref[...]
    n_doubling = int(math.log2(chunk)) - 1

    def blockdiag(x, mask):
        reps = mask.shape[0] // x.shape[0]
        return jnp.where(mask, jnp.concatenate([x] * reps, axis=0), 0.0)

    seqs = list(range(ns))

    def each(f, *cols):
        return [f(*[c[s] for c in cols]) for s in seqs]

    def phase_a(c0):
        bgs = each(lambda s: _split(bg[s, c0:c0 + chunk, :]), seqs)
        selm = each(lambda b: _dot(b[0], sel) + _dot(b[1], sel), bgs)
        cum = each(lambda m: _dot_lhs_exact(tril_f, m[:, 0:hc + G_W]), selm)
        gj_cat = each(lambda m: _dot_lhs_exact(ones_cc, m[:, 0:hc] * triu_cat), selm)
        gi = each(lambda c: c[:, hc:hc + G_W], cum)
        beta = each(lambda m: m[:, hc + G_W:hc + 2 * G_W], selm)
        decay = each(lambda c, gj: jnp.where(tril_cat, jnp.exp(c[:, 0:hc] - gj), 0.0), cum, gj_cat)
        egi = each(jnp.exp, gi)
        q = each(lambda s: ys[s, c0:c0 + chunk, 0:G_W], seqs)
        k = each(lambda s: ys[s, c0:c0 + chunk, G_W:2 * G_W], seqs)
        v = each(lambda s: ys[s, c0:c0 + chunk, 2 * G_W:3 * G_W], seqs)
        kb = each(lambda a, b: a * b, k, beta)
        bdk = each(lambda a: blockdiag(a, bd_cd).astype(BF16), k)
        la = each(lambda a, b, c: _dot_nt(jnp.concatenate([a, b], axis=0).astype(BF16), c), kb, q, bdk)
        a_in = each(lambda l, d: (l[chunk:2 * chunk] * d).astype(BF16), la, decay)
        pw = each(lambda l, d: -jnp.where(strict_cat, l[0:chunk] * d, 0.0), la, decay)
        tinv = each(lambda p: eye_cat + p, pw)
        bd = each(lambda p: _split(blockdiag(p, bd_cc)), pw)
        for _ in range(n_doubling):
            pw = each(lambda p, b: _dot_split(p, b[0], b[1]), pw, bd)
            bd = each(lambda p: _split(blockdiag(p, bd_cc)), pw)
            tinv = each(lambda ti, b: ti + _dot_split(ti, b[0], b[1]), tinv, bd)
        rhs = each(lambda vv, b, kk, e: jnp.concatenate(
            [blockdiag(vv * b, bd_cd), blockdiag(kk * e, bd_cd)], axis=1).astype(BF16), v, beta, kb, egi)
        uw = each(lambda ti, r: _dot(ti.astype(BF16), r), tinv, rhs)
        return dict(uw=uw, a=a_in, qe=each(lambda a, e: a * e, q, egi),
                    kd=each(lambda a, g: (a * jnp.exp(g[chunk - 1:chunk, :] - g)).astype(BF16), k, gi),
                    gl=each(lambda g: jnp.exp(g[chunk - 1:chunk, :]), gi))

    def phase_b(c0, a):
        s_cat = each(lambda s: s_scr[s], seqs)
        bds = each(lambda x: blockdiag(x, bd_dd).astype(BF16), s_cat)
        ws = each(lambda uw, qe, b: _dot(jnp.concatenate([uw[:, G_W:2 * G_W], qe], axis=0).astype(BF16), b),
                  a["uw"], a["qe"], bds)
        v_new = each(lambda uw, w: uw[:, 0:G_W] - w[0:chunk], a["uw"], ws)
        bdv = each(lambda x: blockdiag(x, bd_cd).astype(BF16), v_new)
        o = each(lambda w, ai, b: w[chunk:2 * chunk] + _dot(ai, b), ws, a["a"], bdv)
        v16 = each(lambda x: x.astype(BF16), v_new)
        n_out = min(tt, chunk)
        for h in range(G_HEADS):
            hs = slice(h * G_DV, (h + 1) * G_DV)
            for s in seqs:
                s_scr[s, :, hs] = (s_cat[s][:, hs] * a["gl"][s][:, hs]
                                   + _dot_tn(a["kd"][s][:, hs], v16[s][:, hs]))
            for s in seqs:
                zz = seq_in[s][1][c0:c0 + n_out, hs]
                o_ref[s, c0:c0 + n_out, hs] = (_rms(o[s][0:n_out, hs], ng) * _silu(zz)).astype(o_ref.dtype)

    n_chunks = rows // chunk
    nxt = phase_a(0)
    for c in range(n_chunks):
        cur = nxt
        if c + 1 < n_chunks:
            nxt = phase_a((c + 1) * chunk)
        phase_b(c * chunk, cur)

    @pl.when(t == nt - 1)
    def _():
        for s in range(ns):
            for h in range(G_HEADS):
                sout_ref[s, h] = s_scr[s, :, h * G_DV:(h + 1) * G_DV]


def _gdn(u_all, gba_all, row0, nb, t_len, conv_state, s0, conv_w, ab, norm_g, name):
    chunk = GDN_CHUNK
    if t_len >= chunk:
        tt = _tile(t_len, 256, chunk)
        rows = tt
    else:
        tt = t_len
        rows = chunk
    nt = t_len // tt
    rb0 = row0 // tt
    w3 = 3 * G_W
    ns = GDN_SEQS_PER_STEP if nb % GDN_SEQS_PER_STEP == 0 else 1

    def seq_specs(s):
        blk = lambda b, t: rb0 + (b * ns + s) * nt + t
        return [pl.BlockSpec((tt, w3), lambda b, t: (blk(b, t), C_GQKV // w3)),
                pl.BlockSpec((tt, G_W), lambda b, t: (blk(b, t), C_GZ // G_W)),
                pl.BlockSpec((tt, LANE), lambda b, t: (blk(b, t), 0))]

    out, s_new = pl.pallas_call(
        functools.partial(_gdn_kernel, ns=ns, tt=tt, chunk=chunk, rows=rows),
        out_shape=(jax.ShapeDtypeStruct((nb, t_len, G_W), BF16),
                   jax.ShapeDtypeStruct((nb, G_HEADS, G_DK, G_DV), F32)),
        grid=(nb // ns, nt),
        in_specs=[spec for s in range(ns) for spec in seq_specs(s)] + [
            pl.BlockSpec((ns, CONV_W - 1, w3), lambda b, t: (b, 0, 0)),
            pl.BlockSpec((ns, G_HEADS, G_DK, G_DV), lambda b, t: (b, 0, 0, 0)),
            pl.BlockSpec((CONV_W, w3), lambda b, t: (0, 0)),
            pl.BlockSpec((2, LANE), lambda b, t: (0, 0)),
            pl.BlockSpec((1, G_DV), lambda b, t: (0, 0)),
        ],
        out_specs=(pl.BlockSpec((ns, tt, G_W), lambda b, t: (b, t, 0)),
                   pl.BlockSpec((ns, G_HEADS, G_DK, G_DV), lambda b, t: (b, 0, 0, 0))),
        scratch_shapes=[pltpu.VMEM((ns, SUBLANE + rows, w3), F32),
                        pltpu.VMEM((ns, rows, w3), F32),
                        pltpu.VMEM((ns, rows, LANE), F32),
                        pltpu.VMEM((ns, G_DK, G_HEADS * G_DV), F32)],
        compiler_params=_params("parallel", "arbitrary"),
        name=name,
    )(*([u_all, u_all, gba_all] * ns), conv_state, s0, conv_w, ab, norm_g)
    return out.reshape(nb * t_len, G_W), s_new


def _gelu_tanh(x):
    return 0.5 * x * (1.0 + jnp.tanh(math.sqrt(2.0 / math.pi) * (x + 0.044715 * (x * x * x))))


def _lru_kernel(lx_ref, ly_ref, cs_ref, h0_ref, cw_ref, cb_ref, wa_ref, ba_ref, wx_ref, bx_ref,
                lam_ref, o_ref, hout_ref, xe, h_scr, *, tt):
    t = pl.program_id(1)
    nt = pl.num_programs(1)

    @pl.when(t == 0)
    def _():
        h_scr[...] = jnp.broadcast_to(h0_ref[...], h_scr.shape)
        xe[0:SUBLANE, :] = jnp.zeros((SUBLANE, L_WIDTH), F32)
        xe[SUBLANE - (CONV_W - 1):SUBLANE, :] = cs_ref[...]

    xe[SUBLANE:SUBLANE + tt, :] = lx_ref[...]
    cw = cw_ref[...]
    base = SUBLANE - (CONV_W - 1)
    xc = xe[base:base + tt, :] * cw[0:1, :]
    for j in range(1, CONV_W):
        xc = xc + xe[base + j:base + j + tt, :] * cw[j:j + 1, :]
    xe[0:SUBLANE, :] = xe[tt:tt + SUBLANE, :]
    xc = xc + cb_ref[...]
    x16 = xc.astype(BF16)
    r = _sigmoid(_dot(x16, wa_ref[...]) + ba_ref[...])
    i = _sigmoid(_dot(x16, wx_ref[...]) + bx_ref[...])
    log_a = -LRU_C * r * _softplus(-lam_ref[...])
    a = jnp.exp(log_a)
    b = jnp.sqrt(jnp.tanh(-log_a) * (a * a + 1.0)) * (i * xc)
    row = lax.broadcasted_iota(jnp.int32, (tt, 1), 0)
    s = 1
    while s < tt:
        keep = row >= s
        a_prev = jnp.where(keep, pltpu.roll(a, s, 0), 1.0)
        b_prev = jnp.where(keep, pltpu.roll(b, s, 0), 0.0)
        b = a * b_prev + b
        a = a * a_prev
        s *= 2
    h = a * h_scr[0:1, :] + b
    h_last = h[tt - 1:tt, :]
    h_scr[...] = jnp.broadcast_to(h_last, h_scr.shape)
    o_ref[...] = (h * _gelu_tanh(ly_ref[...])).astype(o_ref.dtype)

    @pl.when(t == nt - 1)
    def _():
        hout_ref[...] = h_last


def _lru(u_all, row0, nb, t_len, conv_state, h0, cw, cb, wa, ba, wx, bx, lam, name):
    tt = _tile(t_len, 256)
    nt = t_len // tt
    rb0 = row0 // tt
    vec = lambda: pl.BlockSpec((1, L_WIDTH), lambda b, t: (0, 0))
    out, h_last = pl.pallas_call(
        functools.partial(_lru_kernel, tt=tt),
        out_shape=(jax.ShapeDtypeStruct((nb * t_len, L_WIDTH), BF16),
                   jax.ShapeDtypeStruct((nb, 1, L_WIDTH), F32)),
        grid=(nb, nt),
        in_specs=[
            pl.BlockSpec((tt, L_WIDTH), lambda b, t: (rb0 + b * nt + t, C_LX // L_WIDTH)),
            pl.BlockSpec((tt, L_WIDTH), lambda b, t: (rb0 + b * nt + t, C_LY // L_WIDTH)),
            pl.BlockSpec((None, CONV_W - 1, L_WIDTH), lambda b, t: (b, 0, 0)),
            pl.BlockSpec((None, 1, L_WIDTH), lambda b, t: (b, 0, 0)),
            pl.BlockSpec((CONV_W, L_WIDTH), lambda b, t: (0, 0)),
            vec(),
            pl.BlockSpec((L_WIDTH, L_WIDTH), lambda b, t: (0, 0)),
            vec(),
            pl.BlockSpec((L_WIDTH, L_WIDTH), lambda b, t: (0, 0)),
            vec(),
            vec(),
        ],
        out_specs=(pl.BlockSpec((tt, L_WIDTH), lambda b, t: (b * nt + t, 0)),
                   pl.BlockSpec((None, 1, L_WIDTH), lambda b, t: (b, 0, 0))),
        scratch_shapes=[pltpu.VMEM((SUBLANE + tt, L_WIDTH), F32),
                        pltpu.VMEM((SUBLANE, L_WIDTH), F32)],
        compiler_params=_params("parallel", "arbitrary"),
        name=name,
    )(u_all, u_all, conv_state, h0.reshape(nb, 1, L_WIDTH), cw, cb, wa, ba, wx, bx, lam)
    return out, h_last.reshape(nb, L_WIDTH)


def _t5_bucket(dist):
    n = jnp.maximum(dist, 0)
    max_exact = NUM_BUCKETS // 2
    nf = jnp.maximum(n, 1).astype(F32)
    large = max_exact + (jnp.log(nf / max_exact) / math.log(MAX_DISTANCE / max_exact)
                         * (NUM_BUCKETS - max_exact)).astype(jnp.int32)
    large = jnp.minimum(large, NUM_BUCKETS - 1)
    return jnp.where(n < max_exact, n, large)


def _lambda_init(layer):
    return 0.8 - 0.6 * math.exp(-0.3 * layer)


def _diff_lambda(lp, lam_init):
    s1 = jnp.sum(lp[0:1, :] * lp[1:2, :], axis=-1, keepdims=True)
    s2 = jnp.sum(lp[2:3, :] * lp[3:4, :], axis=-1, keepdims=True)
    return jnp.exp(s1) - jnp.exp(s2) + lam_init


def _softmax_step(s, m_ref, l_ref, acc_ref, pv):
    m_prev = m_ref[...]
    m_new = jnp.maximum(m_prev, jnp.max(s, axis=-1, keepdims=True))
    alpha = jnp.exp2(m_prev - m_new)
    p = jnp.exp2(s - m_new)
    l_ref[...] = alpha * l_ref[...] + jnp.sum(p, axis=-1, keepdims=True)
    acc_ref[...] = alpha * acc_ref[...] + pv(p)
    m_ref[...] = m_new


LOG2E = math.log2(math.e)
ATTN_COL_CHUNK = LANE


def _attn_prompt_kernel(qi_ref, ki_ref, rb_ref, q_ref, k_ref, vt_ref, lp_ref, g_ref, o_ref,
                        qs, m_scr, l_scr, acc, tab, *, tq, lam_init):
    h = pl.program_id(1)
    qi = qi_ref[pl.program_id(2)]
    ki = ki_ref[pl.program_id(2)]
    cw = min(ATTN_COL_CHUNK, tq)

    @pl.when((qi == 0) & (ki == 0))
    def _():
        kj = lax.broadcasted_iota(jnp.int32, (tq, tq), 0)
        qq = lax.broadcasted_iota(jnp.int32, (tq, tq), 1)
        far = rb_ref[h * NUM_BUCKETS + NUM_BUCKETS - 1]
        for which in range(2):
            dist = qq - kj + which * tq
            bucket = _t5_bucket(dist)
            val = jnp.zeros((tq, tq), F32)
            for b in range(NUM_BUCKETS - 2, -1, -1):
                val = jnp.where(bucket == b, (rb_ref[h * NUM_BUCKETS + b] - far) * LOG2E, val)
            if which == 0:
                val = jnp.where(dist >= 0, val, NEG)
            tab[which] = val

    @pl.when(ki == 0)
    def _():
        q = q_ref[...] * ((A_DQK ** -0.5) * LOG2E)
        lane = lax.broadcasted_iota(jnp.int32, (tq, 2 * A_DQK), 1)
        qs[0:tq, :] = jnp.where(lane < A_DQK, q, 0.0).astype(BF16)
        qs[tq:2 * tq, :] = jnp.where(lane >= A_DQK, q, 0.0).astype(BF16)
        m_scr[...] = jnp.full(m_scr.shape, NEG, F32)
        l_scr[...] = jnp.zeros(l_scr.shape, F32)
        acc[...] = jnp.zeros(acc.shape, F32)

    def step(which):
        k16 = k_ref[...].astype(BF16)
        vt16 = vt_ref[...]
        n_chunks = 2 * tq // cw
        s_next = _dot_nt(k16, qs[0:cw, :])
        for c in range(n_chunks):
            cols = slice(c * cw, (c + 1) * cw)
            s = s_next
            if c + 1 < n_chunks:
                s_next = _dot_nt(k16, qs[(c + 1) * cw:(c + 2) * cw, :])
            if which is not None:
                q0 = (c * cw) % tq
                s = s + tab[which, :, q0:q0 + cw]
            m_prev = m_scr[0:1, cols]
            m_new = jnp.maximum(m_prev, jnp.max(s, axis=0, keepdims=True))
            alpha = jnp.exp2(m_prev - m_new)
            p = jnp.exp2(s - m_new)
            l_scr[0:1, cols] = alpha * l_scr[0:1, cols] + jnp.sum(p, axis=0, keepdims=True)
            acc[:, cols] = alpha * acc[:, cols] + _dot(vt16, p.astype(BF16))
            m_scr[0:1, cols] = m_new

    @pl.when(ki < qi - 1)
    def _():
        step(None)

    @pl.when(ki == qi - 1)
    def _():
        step(1)

    @pl.when(ki == qi)
    def _():
        step(0)
        o = acc[...] / l_scr[0:1, :]
        lam = _diff_lambda(lp_ref[...], lam_init)
        a = o[:, 0:tq] - lam * o[:, tq:2 * tq]
        a = a * lax.rsqrt(jnp.mean(a * a, axis=0, keepdims=True) + EPS)
        o_ref[...] = (a.T * g_ref[...] * (1.0 - lam_init)).astype(o_ref.dtype)


def _attn_prompt(u_all, nb, t_len, rel_bias_flat, a_lambda, subln_g, layer, name):
    tq = _tile(t_len, 512, LANE)
    nq = t_len // tq
    lam_init = _lambda_init(layer)
    mp = nb * t_len
    vt = u_all[:mp, C_AV:C_AV + A_W].astype(BF16).reshape(nb, t_len, A_HEADS, A_DV)
    vt = vt.transpose(0, 2, 3, 1).reshape(nb * A_HEADS * A_DV, t_len)
    pairs = [(qi, ki) for qi in range(nq) for ki in range(qi + 1)]
    qi_of = jnp.array([p[0] for p in pairs], jnp.int32)
    ki_of = jnp.array([p[1] for p in pairs], jnp.int32)
    grid_spec = pltpu.PrefetchScalarGridSpec(
        num_scalar_prefetch=2,
        grid=(nb, A_HEADS, len(pairs)),
        in_specs=[
            pl.BlockSpec(memory_space=pltpu.SMEM),
            pl.BlockSpec((tq, A_DV), lambda b, h, p, qi, ki: (b * nq + qi[p], C_AQ // A_DV + h)),
            pl.BlockSpec((tq, A_DV), lambda b, h, p, qi, ki: (b * nq + ki[p], C_AK // A_DV + h)),
            pl.BlockSpec((A_DV, tq), lambda b, h, p, qi, ki: (b * A_HEADS + h, ki[p])),
            pl.BlockSpec((4, A_DQK), lambda b, h, p, qi, ki: (0, 0)),
            pl.BlockSpec((1, A_DV), lambda b, h, p, qi, ki: (0, 0)),
        ],
        out_specs=pl.BlockSpec((tq, A_DV), lambda b, h, p, qi, ki: (b * nq + qi[p], h)),
        scratch_shapes=[pltpu.VMEM((2 * tq, A_DV), BF16),
                        pltpu.VMEM((SUBLANE, 2 * tq), F32),
                        pltpu.VMEM((SUBLANE, 2 * tq), F32),
                        pltpu.VMEM((A_DV, 2 * tq), F32),
                        pltpu.VMEM((2, tq, tq), F32)],
    )
    return pl.pallas_call(
        functools.partial(_attn_prompt_kernel, tq=tq, lam_init=lam_init),
        out_shape=jax.ShapeDtypeStruct((mp, A_W), BF16),
        grid_spec=grid_spec,
        compiler_params=_params("arbitrary", "arbitrary", "arbitrary"),
        name=name,
    )(qi_of, ki_of, rel_bias_flat, u_all, u_all, vt, a_lambda, subln_g)


def _attn_sample_kernel(pt_ref, rb_ref, q_ref, kn_ref, vn_ref, lp_ref, g_ref, *rest,
                        n_pg, t_len, page, past, lam_init):
    kp = rest[0:n_pg]
    vp = rest[n_pg:2 * n_pg]
    o_ref = rest[2 * n_pg]
    qs, m_scr, l_scr, acc, knp, vnp = rest[2 * n_pg + 1:]
    s_id = pl.program_id(1)
    n_steps = pl.num_programs(1)
    n_cache = n_steps - 1
    rows = A_HEADS * 2 * t_len
    cols = page * A_HEADS
    row = lax.broadcasted_iota(jnp.int32, (rows, 1), 0)
    row_h = row // (2 * t_len)
    row_q = row % t_len

    def head_col(b):
        c = jnp.full((rows, 1), rb_ref[b], F32)
        for hh in range(1, A_HEADS):
            c = jnp.where(row_h == hh, rb_ref[hh * NUM_BUCKETS + b], c)
        return c

    def exact_bias(dist):
        bucket = _t5_bucket(dist)
        far = head_col(NUM_BUCKETS - 1)
        val = jnp.zeros(dist.shape, F32)
        for b in range(NUM_BUCKETS - 2, -1, -1):
            val = jnp.where(bucket == b, (head_col(b) - far) * LOG2E, val)
        return val

    @pl.when(s_id == 0)
    def _():
        q = q_ref[...] * ((A_DQK ** -0.5) * LOG2E)
        lane = lax.broadcasted_iota(jnp.int32, (t_len, A_DV), 1)
        for hh in range(A_HEADS):
            qh = q[:, hh * A_DV:(hh + 1) * A_DV]
            qs[hh * 2 * t_len:hh * 2 * t_len + t_len, :] = jnp.where(lane < A_DQK, qh, 0.0).astype(BF16)
            qs[hh * 2 * t_len + t_len:(hh + 1) * 2 * t_len, :] = jnp.where(lane >= A_DQK, qh, 0.0).astype(BF16)
        m_scr[...] = jnp.full(m_scr.shape, NEG, F32)
        l_scr[...] = jnp.zeros(l_scr.shape, F32)
        acc[...] = jnp.zeros(acc.shape, F32)

    def head_rows(refs, hh):
        return jnp.concatenate([r[pl.ds(hh, page, stride=A_HEADS), :].astype(BF16) for r in refs], axis=0)

    def cache_step(exact):
        n_keys = n_pg * page
        s_all = jnp.concatenate(
            [_dot_nt(qs[hh * 2 * t_len:(hh + 1) * 2 * t_len, :], head_rows(kp, hh)) for hh in range(A_HEADS)],
            axis=0)
        if exact:
            k_pos = (s_id - 1) * n_keys + lax.broadcasted_iota(jnp.int32, (rows, n_keys), 1)
            s_all = s_all + exact_bias(past + row_q - k_pos)

        def pv(p):
            return jnp.concatenate(
                [_dot(p[hh * 2 * t_len:(hh + 1) * 2 * t_len, :].astype(BF16), head_rows(vp, hh))
                 for hh in range(A_HEADS)], axis=0)

        _softmax_step(s_all, m_scr, l_scr, acc, pv)

    @pl.when((s_id >= 1) & (s_id < n_cache))
    def _():
        cache_step(False)

    @pl.when(s_id == n_cache)
    def _():
        cache_step(True)

    @pl.when(s_id == 0)
    def _():
        knp[...] = jnp.zeros(knp.shape, BF16)
        vnp[...] = jnp.zeros(vnp.shape, BF16)
        kn = kn_ref[...]
        vn = vn_ref[...]
        for hh in range(A_HEADS):
            knp[hh, 0:t_len, :] = kn[:, hh * A_DV:(hh + 1) * A_DV].astype(BF16)
            vnp[hh, 0:t_len, :] = vn[:, hh * A_DV:(hh + 1) * A_DV].astype(BF16)
        kcol = lax.broadcasted_iota(jnp.int32, (rows, page), 1)
        dist = row_q - kcol
        bias = jnp.where((dist >= 0) & (kcol < t_len), exact_bias(dist), NEG)
        parts = []
        for hh in range(A_HEADS):
            parts.append(_dot_nt(qs[hh * 2 * t_len:(hh + 1) * 2 * t_len, :], knp[hh]))
        s = jnp.concatenate(parts, axis=0) + bias

        def pv(p):
            outs = []
            for hh in range(A_HEADS):
                outs.append(_dot(p[hh * 2 * t_len:(hh + 1) * 2 * t_len, :].astype(BF16), vnp[hh]))
            return jnp.concatenate(outs, axis=0)

        _softmax_step(s, m_scr, l_scr, acc, pv)

    @pl.when(s_id == n_cache)
    def _():
        o = acc[...] / l_scr[...]
        lam = _diff_lambda(lp_ref[...], lam_init)
        g = g_ref[...]
        for hh in range(A_HEADS):
            r0 = hh * 2 * t_len
            a = o[r0:r0 + t_len, :] - lam * o[r0 + t_len:r0 + 2 * t_len, :]
            o_ref[:, hh * A_DV:(hh + 1) * A_DV] = (_rms(a, g) * (1.0 - lam_init)).astype(o_ref.dtype)


def _attn_sample(u_all, row0, nb, t_len, cache_k, cache_v, page_table, rel_bias_flat, a_lambda,
                 subln_g, layer, name):
    depth, n_pool, page = cache_k.shape[0], cache_k.shape[1], cache_k.shape[2]
    n_pages = page_table.shape[1]
    n_pg = _tile(n_pages, 16, 1)
    n_cache = n_pages // n_pg
    past = n_pages * page
    rb0 = row0 // t_len
    rows = A_HEADS * 2 * t_len
    ck = cache_k.reshape(depth, n_pool, page * A_HEADS, A_DV)
    cv = cache_v.reshape(depth, n_pool, page * A_HEADS, A_DV)
    pt = page_table.reshape(-1)

    def page_spec(i):
        def imap(b, s, pt_ref):
            step = jnp.maximum(s - 1, 0)
            return (layer, pt_ref[b * n_pages + step * n_pg + i], 0, 0)
        return pl.BlockSpec((None, None, page * A_HEADS, A_DV), imap)

    grid_spec = pltpu.PrefetchScalarGridSpec(
        num_scalar_prefetch=1,
        grid=(nb, n_cache + 1),
        in_specs=[
            pl.BlockSpec(memory_space=pltpu.SMEM),
            pl.BlockSpec((t_len, A_W), lambda b, s, pt_ref: (rb0 + b, C_AQ // A_W)),
            pl.BlockSpec((t_len, A_W), lambda b, s, pt_ref: (rb0 + b, C_AK // A_W)),
            pl.BlockSpec((t_len, A_W), lambda b, s, pt_ref: (rb0 + b, C_AV // A_W)),
            pl.BlockSpec((4, A_DQK), lambda b, s, pt_ref: (0, 0)),
            pl.BlockSpec((1, A_DV), lambda b, s, pt_ref: (0, 0)),
        ] + [page_spec(i) for i in range(n_pg)] + [page_spec(i) for i in range(n_pg)],
        out_specs=pl.BlockSpec((t_len, A_W), lambda b, s, pt_ref: (b, 0)),
        scratch_shapes=[pltpu.VMEM((rows, A_DV), BF16),
                        pltpu.VMEM((rows, 1), F32),
                        pltpu.VMEM((rows, 1), F32),
                        pltpu.VMEM((rows, A_DV), F32),
                        pltpu.VMEM((A_HEADS, page, A_DV), BF16),
                        pltpu.VMEM((A_HEADS, page, A_DV), BF16)],
    )
    return pl.pallas_call(
        functools.partial(_attn_sample_kernel, n_pg=n_pg, t_len=t_len, page=page, past=past,
                          lam_init=_lambda_init(layer)),
        out_shape=jax.ShapeDtypeStruct((nb * t_len, A_W), BF16),
        grid_spec=grid_spec,
        compiler_params=_params("arbitrary", "arbitrary"),
        name=name,
    )(pt, rel_bias_flat, u_all, u_all, u_all, a_lambda, subln_g, *([ck] * n_pg), *([cv] * n_pg))


def _merge_kernel(x_ref, og_ref, oa_ref, ol_ref, gt_ref, wg_ref, wa_ref, wl_ref, wo_ref, o_ref):
    d = x_ref.shape[1]
    gt = gt_ref[...].astype(F32)
    m = gt[:, 0:d] * _dot(og_ref[...], wg_ref[...])
    m = m + gt[:, d:2 * d] * _dot(oa_ref[...], wa_ref[...])
    m = m + gt[:, 2 * d:3 * d] * _dot(ol_ref[...], wl_ref[...])
    o_ref[...] = x_ref[...] + _dot(m.astype(BF16), wo_ref[...])


def _merge(x, o_g, o_a, o_l, gates, wg, wa, wl, wo, name):
    m, d = x.shape
    tm = _tile(m, 640)
    row = lambda w: pl.BlockSpec((tm, w), lambda i: (i, 0))
    full = lambda a: pl.BlockSpec(a.shape, lambda i: (0, 0))
    return pl.pallas_call(
        _merge_kernel,
        out_shape=jax.ShapeDtypeStruct((m, d), F32),
        grid=(m // tm,),
        in_specs=[row(d), row(G_W), row(A_W), row(L_WIDTH), row(3 * d),
                  full(wg), full(wa), full(wl), full(wo)],
        out_specs=row(d),
        compiler_params=_params("parallel"),
        name=name,
    )(x, o_g, o_a, o_l, gates, wg, wa, wl, wo)


def _ffn_kernel(x_ref, g_ref, w1_ref, w3_ref, w2_ref, o_ref, h_scr, acc):
    f = pl.program_id(1)

    @pl.when(f == 0)
    def _():
        x = x_ref[...]
        h_scr[...] = _rms(x, g_ref[...]).astype(BF16)
        acc[...] = x

    h = h_scr[...]
    t = _silu(_dot(h, w1_ref[...])) * _dot(h, w3_ref[...])
    acc[...] += _dot(t.astype(BF16), w2_ref[...])

    @pl.when(f == pl.num_programs(1) - 1)
    def _():
        o_ref[...] = acc[...]


def _ffn(x, g, w1, w3, w2, name):
    m, d = x.shape
    ff = w1.shape[1]
    tm = _tile(m, 1280)
    tf = _tile(ff, 512, LANE)
    return pl.pallas_call(
        _ffn_kernel,
        out_shape=jax.ShapeDtypeStruct((m, d), F32),
        grid=(m // tm, ff // tf),
        in_specs=[
            pl.BlockSpec((tm, d), lambda i, f: (i, 0)),
            pl.BlockSpec((1, d), lambda i, f: (0, 0)),
            pl.BlockSpec((d, tf), lambda i, f: (0, f)),
            pl.BlockSpec((d, tf), lambda i, f: (0, f)),
            pl.BlockSpec((tf, d), lambda i, f: (f, 0)),
        ],
        out_specs=pl.BlockSpec((tm, d), lambda i, f: (i, 0)),
        scratch_shapes=[pltpu.VMEM((tm, d), BF16), pltpu.VMEM((tm, d), F32)],
        compiler_params=_params("parallel", "arbitrary"),
        name=name,
    )(x, g, w1, w3, w2)


def _router_kernel(x_ref, g_ref, wr_ref, h_ref, o_ref):
    h = _rms(x_ref[...], g_ref[...])
    h_ref[...] = h.astype(BF16)
    wr_hi, wr_lo = _split(wr_ref[...])
    logits = _dot_split(h, wr_hi, wr_lo)
    lane = lax.broadcasted_iota(jnp.int32, logits.shape, 1)
    logits = jnp.where(lane < N_EXPERTS, logits, -jnp.inf)
    m1 = jnp.max(logits, axis=-1, keepdims=True)
    i1 = jnp.min(jnp.where(logits == m1, lane, LANE), axis=-1, keepdims=True)
    rest = jnp.where(lane == i1, -jnp.inf, logits)
    m2 = jnp.max(rest, axis=-1, keepdims=True)
    i2 = jnp.min(jnp.where(rest == m2, lane, LANE), axis=-1, keepdims=True)
    e2 = jnp.exp(m2 - m1)
    den = 1.0 + e2
    o_ref[...] = jnp.where(lane == 0, i1.astype(F32),
                           jnp.where(lane == 1, i2.astype(F32),
                                     jnp.where(lane == 2, 1.0 / den, jnp.where(lane == 3, e2 / den, 0.0))))


def _router(x, g, wr, name):
    m, d = x.shape
    tm = _tile(m, 1280)
    return pl.pallas_call(
        _router_kernel,
        out_shape=(jax.ShapeDtypeStruct((m, d), BF16), jax.ShapeDtypeStruct((m, LANE), F32)),
        grid=(m // tm,),
        in_specs=[pl.BlockSpec((tm, d), lambda i: (i, 0)),
                  pl.BlockSpec((1, d), lambda i: (0, 0)),
                  pl.BlockSpec((d, LANE), lambda i: (0, 0))],
        out_specs=(pl.BlockSpec((tm, d), lambda i: (i, 0)), pl.BlockSpec((tm, LANE), lambda i: (i, 0))),
        compiler_params=_params("parallel"),
        name=name,
    )(x, g, wr)


MOE_ROW_BLOCK = 640


def _expert_ffn_kernel(be_ref, na_ref, x_ref, w1_ref, w3_ref, w2_ref, o_ref, acc):
    b = pl.program_id(0)
    f = pl.program_id(1)
    active = b < na_ref[0]

    def partial_out():
        x = x_ref[...]
        t = _silu(_dot(x, w1_ref[...])) * _dot(x, w3_ref[...])
        return _dot(t.astype(BF16), w2_ref[...])

    @pl.when(active & (f == 0))
    def _():
        acc[...] = partial_out()

    @pl.when(active & (f > 0))
    def _():
        acc[...] += partial_out()

    @pl.when(jnp.logical_not(active) & (f == 0))
    def _():
        acc[...] = jnp.zeros(acc.shape, F32)

    @pl.when(f == pl.num_programs(1) - 1)
    def _():
        o_ref[...] = acc[...]


def _expert_ffn(xs, block_expert, n_active, w1, w3, w2, name):
    n, d = xs.shape
    ff = w1.shape[2]
    rb = MOE_ROW_BLOCK
    tf = _tile(ff, 512, LANE)
    grid_spec = pltpu.PrefetchScalarGridSpec(
        num_scalar_prefetch=2,
        grid=(n // rb, ff // tf),
        in_specs=[
            pl.BlockSpec((rb, d), lambda b, f, be, na: (b, 0)),
            pl.BlockSpec((None, d, tf), lambda b, f, be, na: (be[b], 0, f)),
            pl.BlockSpec((None, d, tf), lambda b, f, be, na: (be[b], 0, f)),
            pl.BlockSpec((None, tf, d), lambda b, f, be, na: (be[b], f, 0)),
        ],
        out_specs=pl.BlockSpec((rb, d), lambda b, f, be, na: (b, 0)),
        scratch_shapes=[pltpu.VMEM((rb, d), F32)],
    )
    return pl.pallas_call(
        _expert_ffn_kernel,
        out_shape=jax.ShapeDtypeStruct((n, d), F32),
        grid_spec=grid_spec,
        compiler_params=_params("arbitrary", "arbitrary"),
        name=name,
    )(block_expert, n_active, xs, w1, w3, w2)


def _combine_kernel(x_ref, y1_ref, y2_ref, r_ref, fg_ref, o_ref):
    r = r_ref[...]
    lane = lax.broadcasted_iota(jnp.int32, r.shape, 1)
    w1 = jnp.sum(jnp.where(lane == 2, r, 0.0), axis=-1, keepdims=True)
    w2 = jnp.sum(jnp.where(lane == 3, r, 0.0), axis=-1, keepdims=True)
    o_ref[...] = _rms(x_ref[...] + (w1 * y1_ref[...] + w2 * y2_ref[...]), fg_ref[...])


def _combine(x, y12, route, final_g, name):
    m, d = x.shape
    tm = _tile(m, 640)
    row = lambda w: pl.BlockSpec((tm, w), lambda i: (i, 0))
    return pl.pallas_call(
        _combine_kernel,
        out_shape=jax.ShapeDtypeStruct((m, d), F32),
        grid=(m // tm,),
        in_specs=[row(d), pl.BlockSpec((tm, d), lambda i: (i, 0)), pl.BlockSpec((tm, d), lambda i: (i, 1)),
                  row(LANE), pl.BlockSpec((1, d), lambda i: (0, 0))],
        out_specs=row(d),
        compiler_params=_params("parallel"),
        name=name,
    )(x, y12, y12, route, final_g)


def _moe_routed(x, g, wr, w1, w3, w2, final_g, layer):
    m, d = x.shape
    ne = w1.shape[0]
    rb = MOE_ROW_BLOCK
    h16, route = _router(x, g, wr, f"router_{layer}")
    eid = route[:, 0:TOP_K].astype(jnp.int32).reshape(-1)
    n_assign = TOP_K * m
    onehot = (eid[:, None] == jnp.arange(ne, dtype=jnp.int32)[None, :]).astype(jnp.int32)
    csum = jnp.cumsum(onehot, axis=0)
    counts = csum[n_assign - 1]
    starts = jnp.cumsum(counts) - counts
    padded = (counts + rb - 1) // rb * rb
    pad_end = jnp.cumsum(padded)
    pad_start = pad_end - padded
    rank = jnp.sum(onehot * csum, axis=1) - 1
    pos = pad_start[eid] + rank
    n_rows = (n_assign + rb - 1) // rb * rb + ne * rb
    block_start = jnp.arange(n_rows // rb, dtype=jnp.int32) * rb
    block_expert = jnp.minimum(jnp.searchsorted(pad_end, block_start, side="right"), ne - 1).astype(jnp.int32)
    n_active = (pad_end[ne - 1:ne] // rb).astype(jnp.int32)
    order = jnp.argsort(eid, stable=True)
    row_e = jnp.repeat(block_expert, rb)
    row_r = jnp.arange(n_rows, dtype=jnp.int32) - pad_start[row_e]
    src = order[jnp.minimum(starts[row_e] + row_r, n_assign - 1)] // TOP_K
    src_tok = jnp.where(row_r < counts[row_e], src, 0)
    xs = jnp.take(h16, src_tok, axis=0)
    ys = _expert_ffn(xs, block_expert, n_active, w1, w3, w2, f"moe_{layer}")
    y12 = jnp.take(ys, pos, axis=0).reshape(m, TOP_K * d)
    return _combine(x, y12, route, final_g, f"moe_combine_{layer}")


def _final_norm_kernel(x_ref, g_ref, o_ref):
    o_ref[...] = _rms(x_ref[...], g_ref[...])


def _final_norm(x, g, name):
    m, d = x.shape
    tm = _tile(m, 1280)
    return pl.pallas_call(
        _final_norm_kernel,
        out_shape=jax.ShapeDtypeStruct((m, d), F32),
        grid=(m // tm,),
        in_specs=[pl.BlockSpec((tm, d), lambda i: (i, 0)), pl.BlockSpec((1, d), lambda i: (0, 0))],
        out_specs=pl.BlockSpec((tm, d), lambda i: (i, 0)),
        compiler_params=_params("parallel"),
        name=name,
    )(x, g)


def _split_w_in(w):
    o_gb = 4 * G_W
    o_aq = o_gb + 2 * G_HEADS
    pad = jnp.zeros((w.shape[0], LANE - 2 * G_HEADS), w.dtype)
    return (jnp.concatenate([w[:, :o_gb], w[:, o_aq:]], axis=1),
            jnp.concatenate([w[:, o_gb:o_aq], pad], axis=1))


def _block_diag(w):
    n, d, _ = w.shape
    eye = jnp.eye(n, dtype=w.dtype)
    return (eye[:, None, :, None] * w[:, :, None, :]).reshape(n * d, n * d)


def kernel(x_prompt, x_sample, cache_k, cache_v, page_table, state_gdn, state_gdn_conv, state_lru, state_lru_conv, norm1_g, norm2_g, final_g, w_in, w_gate, b_gate, g_conv_w, g_a_log, g_dt_bias, g_norm_g, a_lambda, a_subln_g, rel_bias, l_conv_w, l_conv_b, l_wa, l_ba, l_wx, l_bx, l_lambda, w_br_g, w_br_a, w_br_l, w_out, f_w1, f_w3, f_w2, m_router, m_w1, m_w3, m_w2):
    bp, tp, d = x_prompt.shape
    bs, ts, _ = x_sample.shape
    depth = w_in.shape[0]
    mp = bp * tp
    ms = bs * ts
    x = jnp.concatenate([x_prompt.reshape(mp, d), x_sample.reshape(ms, d)], axis=0)
    rb_flat = rel_bias.T.reshape(-1)
    row = lambda v: v.reshape(1, -1)

    k_p, v_p, k_s, v_s = [], [], [], []
    sg_p, sg_s, cg_p, cg_s, sl_p, sl_s, cl_p, cl_s = [], [], [], [], [], [], [], []
    y = None
    for l in range(depth):
        w_main, w_gba = _split_w_in(w_in[l])
        u = _norm_matmul(x, row(norm1_g[l]), w_main.astype(BF16), jnp.zeros((1, U_COLS), F32),
                         None, F32, f"in_proj_{l}")
        gba = _norm_matmul(x, row(norm1_g[l]), w_gba.astype(BF16), jnp.zeros((1, LANE), F32),
                           None, F32, f"in_proj_gba_{l}")
        gates = _norm_matmul(x, row(norm1_g[l]), w_gate[l].astype(BF16), row(b_gate[l]),
                             "sigmoid", BF16, f"gate_proj_{l}")

        ab = jnp.zeros((2, LANE), F32)
        ab = ab.at[0, G_HEADS:2 * G_HEADS].set(g_a_log[l]).at[1, G_HEADS:2 * G_HEADS].set(g_dt_bias[l])
        ng = row(g_norm_g[l])
        og_p, s_p = _gdn(u, gba, 0, bp, tp, jnp.zeros((bp, CONV_W - 1, 3 * G_W), F32),
                         jnp.zeros((bp, G_HEADS, G_DK, G_DV), F32), g_conv_w[l], ab, ng, f"gdn_prompt_{l}")
        og_s, s_s = _gdn(u, gba, mp, bs, ts, state_gdn_conv[l], state_gdn[l], g_conv_w[l], ab, ng,
                         f"gdn_sample_{l}")

        sub_g = row(a_subln_g[l])
        oa_p = _attn_prompt(u, bp, tp, rb_flat, a_lambda[l], sub_g, l, f"attn_prompt_{l}")
        oa_s = _attn_sample(u, mp, bs, ts, cache_k, cache_v, page_table, rb_flat, a_lambda[l], sub_g, l,
                            f"attn_sample_{l}")

        wa_bd = _block_diag(l_wa[l]).astype(BF16)
        wx_bd = _block_diag(l_wx[l]).astype(BF16)
        lru_args = (l_conv_w[l], row(l_conv_b[l]), wa_bd, row(l_ba[l]), wx_bd, row(l_bx[l]), row(l_lambda[l]))
        ol_p, h_p = _lru(u, 0, bp, tp, jnp.zeros((bp, CONV_W - 1, L_WIDTH), F32),
                         jnp.zeros((bp, L_WIDTH), F32), *lru_args, f"lru_prompt_{l}")
        ol_s, h_s = _lru(u, mp, bs, ts, state_lru_conv[l], state_lru[l], *lru_args, f"lru_sample_{l}")

        x = _merge(x, jnp.concatenate([og_p, og_s]), jnp.concatenate([oa_p, oa_s]),
                   jnp.concatenate([ol_p, ol_s]), gates, w_br_g[l].astype(BF16), w_br_a[l].astype(BF16),
                   w_br_l[l].astype(BF16), w_out[l].astype(BF16), f"merge_{l}")

        j = l // 2
        if l % 2 == 0:
            x = _ffn(x, row(norm2_g[l]), f_w1[j].astype(BF16), f_w3[j].astype(BF16), f_w2[j].astype(BF16),
                     f"ffn_{l}")
            if l == depth - 1:
                y = _final_norm(x, row(final_g), "final_norm")
        else:
            wr = jnp.zeros((d, LANE), F32).at[:, :N_EXPERTS].set(m_router[j])
            if l != depth - 1:
                raise NotImplementedError("mixture layer is only fused with the final norm")
            y = _moe_routed(x, row(norm2_g[l]), wr, m_w1[j].astype(BF16), m_w3[j].astype(BF16),
                            m_w2[j].astype(BF16), row(final_g), l)

        def tail_rows(c0, width):
            keep = CONV_W - 1
            p = jnp.stack([u[(b + 1) * tp - keep:(b + 1) * tp, c0:c0 + width] for b in range(bp)])
            s = u[mp:, c0:c0 + width].reshape(bs, ts, width)[:, ts - keep:]
            return p, s

        k_p.append(u[:mp, C_AK:C_AK + A_W].reshape(bp, tp, A_HEADS, A_DV))
        v_p.append(u[:mp, C_AV:C_AV + A_W].reshape(bp, tp, A_HEADS, A_DV))
        k_s.append(u[mp:, C_AK:C_AK + A_W].reshape(bs, ts, A_HEADS, A_DV))
        v_s.append(u[mp:, C_AV:C_AV + A_W].reshape(bs, ts, A_HEADS, A_DV))
        sg_p.append(s_p)
        sg_s.append(s_s)
        cp, cs = tail_rows(C_GQKV, 3 * G_W)
        cg_p.append(cp)
        cg_s.append(cs)
        sl_p.append(h_p)
        sl_s.append(h_s)
        cp, cs = tail_rows(C_LX, L_WIDTH)
        cl_p.append(cp)
        cl_s.append(cs)

    st = jnp.stack
    return (y[:mp].reshape(bp, tp, d), y[mp:].reshape(bs, ts, d),
            st(k_p), st(v_p), st(k_s), st(v_s), st(sg_p), st(sg_s), st(cg_p), st(cg_s),
            st(sl_p), st(sl_s), st(cl_p), st(cl_s))
```

```python
import functools
import math

import jax
import jax.numpy as jnp
from jax import lax
from jax.experimental import pallas as pl
from jax.experimental.pallas import tpu as pltpu

F32 = jnp.float32
BF16 = jnp.bfloat16
EPS = 1e-6
NEG = -1e30

G_HEADS, G_DK, G_DV = 4, 128, 128
G_W = G_HEADS * G_DK
GDN_CHUNK = 64
GDN_SEQS_PER_STEP = 2
CONV_W = 4
A_HEADS, A_DQK, A_DV = 4, 64, 128
A_W = A_HEADS * A_DV
L_WIDTH, L_BLOCKS = 512, 8
LRU_C = 8.0
NUM_BUCKETS, MAX_DISTANCE = 32, 128
N_EXPERTS, TOP_K = 8, 2
LANE = 128
SUBLANE = 8
VMEM_LIMIT = 56 * 1024 * 1024

C_GQKV = 0
C_GZ = 3 * G_W
C_AQ = 4 * G_W
C_AK = C_AQ + A_W
C_AV = C_AK + A_W
C_LX = C_AV + A_W
C_LY = C_LX + L_WIDTH
U_COLS = C_LY + L_WIDTH


def _tile(n, target, mult=SUBLANE):
    best = None
    for d in range(mult, min(n, target) + 1, mult):
        if n % d == 0:
            best = d
    return best if best is not None else n


def _params(*sem):
    return pltpu.CompilerParams(dimension_semantics=sem, vmem_limit_bytes=VMEM_LIMIT)


def _dot(a, b):
    return jnp.dot(a, b, preferred_element_type=F32)


def _dot_nt(a, b):
    return lax.dot_general(a, b, (((1,), (1,)), ((), ())), preferred_element_type=F32)


def _dot_tn(a, b):
    return lax.dot_general(a, b, (((0,), (0,)), ((), ())), preferred_element_type=F32)


def _split(x):
    hi = x.astype(BF16)
    return hi, (x - hi.astype(F32)).astype(BF16)


def _dot_split(a, b_hi, b_lo):
    a_hi, a_lo = _split(a)
    return _dot(a_hi, b_hi) + (_dot(a_hi, b_lo) + _dot(a_lo, b_hi))


def _dot_lhs_exact(a16, b):
    b_hi, b_lo = _split(b)
    return _dot(a16, b_hi) + _dot(a16, b_lo)


def _sigmoid(x):
    return 1.0 / (1.0 + jnp.exp(-x))


def _silu(x):
    return x * _sigmoid(x)


def _softplus(x):
    return jnp.maximum(x, 0.0) + jnp.log1p(jnp.exp(-jnp.abs(x)))


def _rms(x, g):
    return x * lax.rsqrt(jnp.mean(x * x, axis=-1, keepdims=True) + EPS) * g


def _norm_mm_kernel(x_ref, g_ref, w_ref, b_ref, o_ref, h_scr, *, act):
    @pl.when(pl.program_id(1) == 0)
    def _():
        h_scr[...] = _rms(x_ref[...], g_ref[...]).astype(BF16)

    acc = _dot(h_scr[...], w_ref[...]) + b_ref[...]
    if act == "sigmoid":
        acc = _sigmoid(acc)
    o_ref[...] = acc.astype(o_ref.dtype)


def _norm_matmul(x, g, w, b, act, out_dtype, name):
    m, k = x.shape
    n = w.shape[1]
    tm = _tile(m, 1280)
    tn = _tile(n, 1536, LANE)
    return pl.pallas_call(
        functools.partial(_norm_mm_kernel, act=act),
        out_shape=jax.ShapeDtypeStruct((m, n), out_dtype),
        grid=(m // tm, n // tn),
        in_specs=[
            pl.BlockSpec((tm, k), lambda i, j: (i, 0)),
            pl.BlockSpec((1, k), lambda i, j: (0, 0)),
            pl.BlockSpec((k, tn), lambda i, j: (0, j)),
            pl.BlockSpec((1, tn), lambda i, j: (0, j)),
        ],
        out_specs=pl.BlockSpec((tm, tn), lambda i, j: (i, j)),
        scratch_shapes=[pltpu.VMEM((tm, k), BF16)],
        compiler_params=_params("parallel", "arbitrary"),
        name=name,
    )(x, g, w, b)


def _gdn_kernel(*refs, ns, tt, chunk, rows):
    seq_in = [refs[3 * i:3 * i + 3] for i in range(ns)]
    cs_ref, s0_ref, cw_ref, ab_ref, ng_ref, o_ref, sout_ref, xe, ys, bg, s_scr = refs[3 * ns:]
    t = pl.program_id(1)
    nt = pl.num_programs(1)
    hc = G_HEADS * chunk

    @pl.when(t == 0)
    def _():
        for s in range(ns):
            for h in range(G_HEADS):
                s_scr[s, :, h * G_DV:(h + 1) * G_DV] = s0_ref[s, h]
            xe[s] = jnp.zeros(xe.shape[1:], F32)
            xe[s, SUBLANE - (CONV_W - 1):SUBLANE, :] = cs_ref[s]

    cw = cw_ref[...]
    ab = ab_ref[...]
    base = SUBLANE - (CONV_W - 1)
    for s in range(ns):
        qkv_ref, _, gba_ref = seq_in[s]
        xe[s, SUBLANE:SUBLANE + tt, :] = qkv_ref[...]
        y = xe[s, base:base + rows, :] * cw[0:1, :]
        for j in range(1, CONV_W):
            y = y + xe[s, base + j:base + j + rows, :] * cw[j:j + 1, :]
        xe[s, 0:SUBLANE, :] = xe[s, tt:tt + SUBLANE, :]

        gba = gba_ref[...]
        if rows > tt:
            gba = jnp.concatenate([gba, jnp.zeros((rows - tt, LANE), F32)], axis=0)
        lane_g = lax.broadcasted_iota(jnp.int32, (rows, LANE), 1)
        bgc = jnp.where(lane_g < G_HEADS, _sigmoid(gba), -jnp.exp(ab[0:1, :]) * _softplus(gba + ab[1:2, :]))
        y = _silu(y)
        if rows > tt:
            live = lax.broadcasted_iota(jnp.int32, (rows, 1), 0) < tt
            y = jnp.where(live, y, 0.0)
            bgc = jnp.where(live, bgc, 0.0)
        bg[s] = bgc
        for h in range(G_HEADS):
            q = y[:, h * G_DK:(h + 1) * G_DK]
            k = y[:, G_W + h * G_DK:G_W + (h + 1) * G_DK]
            ys[s, :, h * G_DK:(h + 1) * G_DK] = (
                q * lax.rsqrt(jnp.sum(q * q, axis=-1, keepdims=True) + 1e-6) * (G_DK ** -0.5))
            ys[s, :, G_W + h * G_DK:G_W + (h + 1) * G_DK] = (
                k * lax.rsqrt(jnp.sum(k * k, axis=-1, keepdims=True) + 1e-6))
        ys[s, :, 2 * G_W:3 * G_W] = y[:, 2 * G_W:3 * G_W]

    def iota(shape, d):
        return lax.broadcasted_iota(jnp.int32, shape, d)

    r_c = iota((chunk, hc), 0)
    j_c = iota((chunk, hc), 1) % chunk
    tril_cat = r_c >= j_c
    strict_cat = r_c > j_c
    eye_cat = (r_c == j_c).astype(F32)
    triu_cat = (r_c <= j_c).astype(F32)
    tril_f = jnp.where(iota((chunk, chunk), 0) >= iota((chunk, chunk), 1), 1.0, 0.0).astype(BF16)
    ones_cc = jnp.ones((chunk, chunk), BF16)
    l_s = iota((LANE, hc + 2 * G_W), 0)
    n_s = iota((LANE, hc + 2 * G_W), 1)
    src_lane = jnp.where(n_s < hc, G_HEADS + n_s // chunk,
                         jnp.where(n_s < hc + G_W, G_HEADS + (n_s - hc) // G_DK,
                                   (n_s - hc - G_W) // G_DK))
    sel = jnp.where(l_s == src_lane, 1.0, 0.0).astype(BF16)
    bd_cc = iota((hc, hc), 0) // chunk == iota((hc, hc), 1) // chunk
    bd_cd = iota((hc, G_W), 0) // chunk == iota((hc, G_W), 1) // G_DK
    bd_dd = iota((G_W, G_W), 0) // G_DK == iota((G_W, G_W), 1) // G_DK
    ng = ng_ref[...]
    n_doubling = int(math.log2(chunk)) - 1

    def blockdiag(x, mask):
        reps = mask.shape[0] // x.shape[0]
        return jnp.where(mask, jnp.concatenate([x] * reps, axis=0), 0.0)

    seqs = list(range(ns))

    def each(f, *cols):
        return [f(*[c[s] for c in cols]) for s in seqs]

    def phase_a(c0):
        bgs = each(lambda s: _split(bg[s, c0:c0 + chunk, :]), seqs)
        selm = each(lambda b: _dot(b[0], sel) + _dot(b[1], sel), bgs)
        cum = each(lambda m: _dot_lhs_exact(tril_f, m[:, 0:hc + G_W]), selm)
        gj_cat = each(lambda m: _dot_lhs_exact(ones_cc, m[:, 0:hc] * triu_cat), selm)
        gi = each(lambda c: c[:, hc:hc + G_W], cum)
        beta = each(lambda m: m[:, hc + G_W:hc + 2 * G_W], selm)
        decay = each(lambda c, gj: jnp.where(tril_cat, jnp.exp(c[:, 0:hc] - gj), 0.0), cum, gj_cat)
        egi = each(jnp.exp, gi)
        q = each(lambda s: ys[s, c0:c0 + chunk, 0:G_W], seqs)
        k = each(lambda s: ys[s, c0:c0 + chunk, G_W:2 * G_W], seqs)
        v = each(lambda s: ys[s, c0:c0 + chunk, 2 * G_W:3 * G_W], seqs)
        kb = each(lambda a, b: a * b, k, beta)
        bdk = each(lambda a: blockdiag(a, bd_cd).astype(BF16), k)
        la = each(lambda a, b, c: _dot_nt(jnp.concatenate([a, b], axis=0).astype(BF16), c), kb, q, bdk)
        a_in = each(lambda l, d: (l[chunk:2 * chunk] * d).astype(BF16), la, decay)
        pw = each(lambda l, d: -jnp.where(strict_cat, l[0:chunk] * d, 0.0), la, decay)
        tinv = each(lambda p: eye_cat + p, pw)
        bd = each(lambda p: _split(blockdiag(p, bd_cc)), pw)
        for _ in range(n_doubling):
            pw = each(lambda p, b: _dot_split(p, b[0], b[1]), pw, bd)
            bd = each(lambda p: _split(blockdiag(p, bd_cc)), pw)
            tinv = each(lambda ti, b: ti + _dot_split(ti, b[0], b[1]), tinv, bd)
        rhs = each(lambda vv, b, kk, e: jnp.concatenate(
            [blockdiag(vv * b, bd_cd), blockdiag(kk * e, bd_cd)], axis=1).astype(BF16), v, beta, kb, egi)
        uw = each(lambda ti, r: _dot(ti.astype(BF16), r), tinv, rhs)
        return dict(uw=uw, a=a_in, qe=each(lambda a, e: a * e, q, egi),
                    kd=each(lambda a, g: (a * jnp.exp(g[chunk - 1:chunk, :] - g)).astype(BF16), k, gi),
                    gl=each(lambda g: jnp.exp(g[chunk - 1:chunk, :]), gi))

    def phase_b(c0, a):
        s_cat = each(lambda s: s_scr[s], seqs)
        bds = each(lambda x: blockdiag(x, bd_dd).astype(BF16), s_cat)
        ws = each(lambda uw, qe, b: _dot(jnp.concatenate([uw[:, G_W:2 * G_W], qe], axis=0).astype(BF16), b),
                  a["uw"], a["qe"], bds)
        v_new = each(lambda uw, w: uw[:, 0:G_W] - w[0:chunk], a["uw"], ws)
        bdv = each(lambda x: blockdiag(x, bd_cd).astype(BF16), v_new)
        o = each(lambda w, ai, b: w[chunk:2 * chunk] + _dot(ai, b), ws, a["a"], bdv)
        v16 = each(lambda x: x.astype(BF16), v_new)
        n_out = min(tt, chunk)
        for h in range(G_HEADS):
            hs = slice(h * G_DV, (h + 1) * G_DV)
            for s in seqs:
                s_scr[s, :, hs] = (s_cat[s][:, hs] * a["gl"][s][:, hs]
                                   + _dot_tn(a["kd"][s][:, hs], v16[s][:, hs]))
            for s in seqs:
                zz = seq_in[s][1][c0:c0 + n_out, hs]
                o_ref[s, c0:c0 + n_out, hs] = (_rms(o[s][0:n_out, hs], ng) * _silu(zz)).astype(o_ref.dtype)

    n_chunks = rows // chunk
    nxt = phase_a(0)
    for c in range(n_chunks):
        cur = nxt
        if c + 1 < n_chunks:
            nxt = phase_a((c + 1) * chunk)
        phase_b(c * chunk, cur)

    @pl.when(t == nt - 1)
    def _():
        for s in range(ns):
            for h in range(G_HEADS):
                sout_ref[s, h] = s_scr[s, :, h * G_DV:(h + 1) * G_DV]


def _gdn(u_all, gba_all, row0, nb, t_len, conv_state, s0, conv_w, ab, norm_g, name):
    chunk = GDN_CHUNK
    if t_len >= chunk:
        tt = _tile(t_len, 256, chunk)
        rows = tt
    else:
        tt = t_len
        rows = chunk
    nt = t_len // tt
    rb0 = row0 // tt
    w3 = 3 * G_W
    ns = GDN_SEQS_PER_STEP if nb % GDN_SEQS_PER_STEP == 0 else 1

    def seq_specs(s):
        blk = lambda b, t: rb0 + (b * ns + s) * nt + t
        return [pl.BlockSpec((tt, w3), lambda b, t: (blk(b, t), C_GQKV // w3)),
                pl.BlockSpec((tt, G_W), lambda b, t: (blk(b, t), C_GZ // G_W)),
                pl.BlockSpec((tt, LANE), lambda b, t: (blk(b, t), 0))]

    out, s_new = pl.pallas_call(
        functools.partial(_gdn_kernel, ns=ns, tt=tt, chunk=chunk, rows=rows),
        out_shape=(jax.ShapeDtypeStruct((nb, t_len, G_W), BF16),
                   jax.ShapeDtypeStruct((nb, G_HEADS, G_DK, G_DV), F32)),
        grid=(nb // ns, nt),
        in_specs=[spec for s in range(ns) for spec in seq_specs(s)] + [
            pl.BlockSpec((ns, CONV_W - 1, w3), lambda b, t: (b, 0, 0)),
            pl.BlockSpec((ns, G_HEADS, G_DK, G_DV), lambda b, t: (b, 0, 0, 0)),
            pl.BlockSpec((CONV_W, w3), lambda b, t: (0, 0)),
            pl.BlockSpec((2, LANE), lambda b, t: (0, 0)),
            pl.BlockSpec((1, G_DV), lambda b, t: (0, 0)),
        ],
        out_specs=(pl.BlockSpec((ns, tt, G_W), lambda b, t: (b, t, 0)),
                   pl.BlockSpec((ns, G_HEADS, G_DK, G_DV), lambda b, t: (b, 0, 0, 0))),
        scratch_shapes=[pltpu.VMEM((ns, SUBLANE + rows, w3), F32),
                        pltpu.VMEM((ns, rows, w3), F32),
                        pltpu.VMEM((ns, rows, LANE), F32),
                        pltpu.VMEM((ns, G_DK, G_HEADS * G_DV), F32)],
        compiler_params=_params("parallel", "arbitrary"),
        name=name,
    )(*([u_all, u_all, gba_all] * ns), conv_state, s0, conv_w, ab, norm_g)
    return out.reshape(nb * t_len, G_W), s_new


def _gelu_tanh(x):
    return 0.5 * x * (1.0 + jnp.tanh(math.sqrt(2.0 / math.pi) * (x + 0.044715 * (x * x * x))))


def _lru_kernel(lx_ref, ly_ref, cs_ref, h0_ref, cw_ref, cb_ref, wa_ref, ba_ref, wx_ref, bx_ref,
                lam_ref, o_ref, hout_ref, xe, h_scr, *, tt):
    t = pl.program_id(1)
    nt = pl.num_programs(1)

    @pl.when(t == 0)
    def _():
        h_scr[...] = jnp.broadcast_to(h0_ref[...], h_scr.shape)
        xe[0:SUBLANE, :] = jnp.zeros((SUBLANE, L_WIDTH), F32)
        xe[SUBLANE - (CONV_W - 1):SUBLANE, :] = cs_ref[...]

    xe[SUBLANE:SUBLANE + tt, :] = lx_ref[...]
    cw = cw_ref[...]
    base = SUBLANE - (CONV_W - 1)
    xc = xe[base:base + tt, :] * cw[0:1, :]
    for j in range(1, CONV_W):
        xc = xc + xe[base + j:base + j + tt, :] * cw[j:j + 1, :]
    xe[0:SUBLANE, :] = xe[tt:tt + SUBLANE, :]
    xc = xc + cb_ref[...]
    x16 = xc.astype(BF16)
    r = _sigmoid(_dot(x16, wa_ref[...]) + ba_ref[...])
    i = _sigmoid(_dot(x16, wx_ref[...]) + bx_ref[...])
    log_a = -LRU_C * r * _softplus(-lam_ref[...])
    a = jnp.exp(log_a)
    b = jnp.sqrt(jnp.tanh(-log_a) * (a * a + 1.0)) * (i * xc)
    row = lax.broadcasted_iota(jnp.int32, (tt, 1), 0)
    s = 1
    while s < tt:
        keep = row >= s
        a_prev = jnp.where(keep, pltpu.roll(a, s, 0), 1.0)
        b_prev = jnp.where(keep, pltpu.roll(b, s, 0), 0.0)
        b = a * b_prev + b
        a = a * a_prev
        s *= 2
    h = a * h_scr[0:1, :] + b
    h_last = h[tt - 1:tt, :]
    h_scr[...] = jnp.broadcast_to(h_last, h_scr.shape)
    o_ref[...] = (h * _gelu_tanh(ly_ref[...])).astype(o_ref.dtype)

    @pl.when(t == nt - 1)
    def _():
        hout_ref[...] = h_last


def _lru(u_all, row0, nb, t_len, conv_state, h0, cw, cb, wa, ba, wx, bx, lam, name):
    tt = _tile(t_len, 256)
    nt = t_len // tt
    rb0 = row0 // tt
    vec = lambda: pl.BlockSpec((1, L_WIDTH), lambda b, t: (0, 0))
    out, h_last = pl.pallas_call(
        functools.partial(_lru_kernel, tt=tt),
        out_shape=(jax.ShapeDtypeStruct((nb * t_len, L_WIDTH), BF16),
                   jax.ShapeDtypeStruct((nb, 1, L_WIDTH), F32)),
        grid=(nb, nt),
        in_specs=[
            pl.BlockSpec((tt, L_WIDTH), lambda b, t: (rb0 + b * nt + t, C_LX // L_WIDTH)),
            pl.BlockSpec((tt, L_WIDTH), lambda b, t: (rb0 + b * nt + t, C_LY // L_WIDTH)),
            pl.BlockSpec((None, CONV_W - 1, L_WIDTH), lambda b, t: (b, 0, 0)),
            pl.BlockSpec((None, 1, L_WIDTH), lambda b, t: (b, 0, 0)),
            pl.BlockSpec((CONV_W, L_WIDTH), lambda b, t: (0, 0)),
            vec(),
            pl.BlockSpec((L_WIDTH, L_WIDTH), lambda b, t: (0, 0)),
            vec(),
            pl.BlockSpec((L_WIDTH, L_WIDTH), lambda b, t: (0, 0)),
            vec(),
            vec(),
        ],
        out_specs=(pl.BlockSpec((tt, L_WIDTH), lambda b, t: (b * nt + t, 0)),
                   pl.BlockSpec((None, 1, L_WIDTH), lambda b, t: (b, 0, 0))),
        scratch_shapes=[pltpu.VMEM((SUBLANE + tt, L_WIDTH), F32),
                        pltpu.VMEM((SUBLANE, L_WIDTH), F32)],
        compiler_params=_params("parallel", "arbitrary"),
        name=name,
    )(u_all, u_all, conv_state, h0.reshape(nb, 1, L_WIDTH), cw, cb, wa, ba, wx, bx, lam)
    return out, h_last.reshape(nb, L_WIDTH)


def _t5_bucket(dist):
    n = jnp.maximum(dist, 0)
    max_exact = NUM_BUCKETS // 2
    nf = jnp.maximum(n, 1).astype(F32)
    large = max_exact + (jnp.log(nf / max_exact) / math.log(MAX_DISTANCE / max_exact)
                         * (NUM_BUCKETS - max_exact)).astype(jnp.int32)
    large = jnp.minimum(large, NUM_BUCKETS - 1)
    return jnp.where(n < max_exact, n, large)


def _lambda_init(layer):
    return 0.8 - 0.6 * math.exp(-0.3 * layer)


def _diff_lambda(lp, lam_init):
    s1 = jnp.sum(lp[0:1, :] * lp[1:2, :], axis=-1, keepdims=True)
    s2 = jnp.sum(lp[2:3, :] * lp[3:4, :], axis=-1, keepdims=True)
    return jnp.exp(s1) - jnp.exp(s2) + lam_init


def _softmax_step(s, m_ref, l_ref, acc_ref, pv):
    m_prev = m_ref[...]
    m_new = jnp.maximum(m_prev, jnp.max(s, axis=-1, keepdims=True))
    alpha = jnp.exp2(m_prev - m_new)
    p = jnp.exp2(s - m_new)
    l_ref[...] = alpha * l_ref[...] + jnp.sum(p, axis=-1, keepdims=True)
    acc_ref[...] = alpha * acc_ref[...] + pv(p)
    m_ref[...] = m_new


LOG2E = math.log2(math.e)
ATTN_COL_CHUNK = LANE


def _attn_prompt_kernel(qi_ref, ki_ref, rb_ref, q_ref, k_ref, vt_ref, lp_ref, g_ref, o_ref,
                        qs, m_scr, l_scr, acc, tab, *, tq, kt, lam_init):
    h = pl.program_id(1)
    qi = qi_ref[pl.program_id(2)]
    ki = ki_ref[pl.program_id(2)]
    cw = min(ATTN_COL_CHUNK, tq)

    @pl.when((qi == 0) & (ki == 0))
    def _():
        kj = lax.broadcasted_iota(jnp.int32, (tq, tq), 0)
        qq = lax.broadcasted_iota(jnp.int32, (tq, tq), 1)
        far = rb_ref[h * NUM_BUCKETS + NUM_BUCKETS - 1]
        for which in range(2):
            dist = qq - kj + which * tq
            bucket = _t5_bucket(dist)
            val = jnp.zeros((tq, tq), F32)
            for b in range(NUM_BUCKETS - 2, -1, -1):
                val = jnp.where(bucket == b, (rb_ref[h * NUM_BUCKETS + b] - far) * LOG2E, val)
            if which == 0:
                val = jnp.where(dist >= 0, val, NEG)
            tab[which] = val

    @pl.when(ki == 0)
    def _():
        q = q_ref[...] * ((A_DQK ** -0.5) * LOG2E)
        lane = lax.broadcasted_iota(jnp.int32, (tq, 2 * A_DQK), 1)
        qs[0:tq, :] = jnp.where(lane < A_DQK, q, 0.0).astype(BF16)
        qs[tq:2 * tq, :] = jnp.where(lane >= A_DQK, q, 0.0).astype(BF16)
        m_scr[...] = jnp.full(m_scr.shape, NEG, F32)
        l_scr[...] = jnp.zeros(l_scr.shape, F32)
        acc[...] = jnp.zeros(acc.shape, F32)

    def step(tiles):
        n_chunks = 2 * tq // cw
        k16 = {hf: k_ref[hf * tq:(hf + 1) * tq, :].astype(BF16) for _, hf in tiles}
        vt16 = {hf: vt_ref[:, hf * tq:(hf + 1) * tq] for _, hf in tiles}
        stream = [(which, hf, c) for which, hf in tiles for c in range(n_chunks)]
        s_next = _dot_nt(k16[stream[0][1]], qs[0:cw, :])
        for i, (which, hf, c) in enumerate(stream):
            cols = slice(c * cw, (c + 1) * cw)
            s = s_next
            if i + 1 < len(stream):
                _, hf_n, c_n = stream[i + 1]
                s_next = _dot_nt(k16[hf_n], qs[c_n * cw:(c_n + 1) * cw, :])
            if which is not None:
                q0 = (c * cw) % tq
                s = s + tab[which, :, q0:q0 + cw]
            m_prev = m_scr[0:1, cols]
            m_new = jnp.maximum(m_prev, jnp.max(s, axis=0, keepdims=True))
            alpha = jnp.exp2(m_prev - m_new)
            p = jnp.exp2(s - m_new)
            l_scr[0:1, cols] = alpha * l_scr[0:1, cols] + jnp.sum(p, axis=0, keepdims=True)
            acc[:, cols] = alpha * acc[:, cols] + _dot(vt16[hf], p.astype(BF16))
            m_scr[0:1, cols] = m_new

    all_far = kt * ki + kt - 1 < qi - 1

    @pl.when(all_far)
    def _():
        step([(None, hf) for hf in range(kt)])

    for hf in range(kt):
        kk = kt * ki + hf

        @pl.when((kk < qi - 1) & jnp.logical_not(all_far))
        def _(hf=hf):
            step([(None, hf)])

        @pl.when(kk == qi - 1)
        def _(hf=hf):
            step([(1, hf)])

        @pl.when(kk == qi)
        def _(hf=hf):
            step([(0, hf)])

    @pl.when(ki == qi // kt)
    def _():
        o = acc[...] / l_scr[0:1, :]
        lam = _diff_lambda(lp_ref[...], lam_init)
        a = o[:, 0:tq] - lam * o[:, tq:2 * tq]
        a = a * lax.rsqrt(jnp.mean(a * a, axis=0, keepdims=True) + EPS)
        o_ref[...] = (a.T * g_ref[...] * (1.0 - lam_init)).astype(o_ref.dtype)


def _attn_prompt(u_all, nb, t_len, rel_bias_flat, a_lambda, subln_g, layer, name):
    tq = _tile(t_len, 512, LANE)
    nq = t_len // tq
    lam_init = _lambda_init(layer)
    mp = nb * t_len
    vt = u_all[:mp, C_AV:C_AV + A_W].astype(BF16).reshape(nb, t_len, A_HEADS, A_DV)
    vt = vt.transpose(0, 2, 3, 1).reshape(nb * A_HEADS * A_DV, t_len)
    kt = 2 if nq % 2 == 0 else 1
    pairs = [(qi, ki) for qi in range(nq) for ki in range(qi // kt + 1)]
    qi_of = jnp.array([p[0] for p in pairs], jnp.int32)
    ki_of = jnp.array([p[1] for p in pairs], jnp.int32)
    grid_spec = pltpu.PrefetchScalarGridSpec(
        num_scalar_prefetch=2,
        grid=(nb, A_HEADS, len(pairs)),
        in_specs=[
            pl.BlockSpec(memory_space=pltpu.SMEM),
            pl.BlockSpec((tq, A_DV), lambda b, h, p, qi, ki: (b * nq + qi[p], C_AQ // A_DV + h)),
            pl.BlockSpec((kt * tq, A_DV), lambda b, h, p, qi, ki: (b * (nq // kt) + ki[p], C_AK // A_DV + h)),
            pl.BlockSpec((A_DV, kt * tq), lambda b, h, p, qi, ki: (b * A_HEADS + h, ki[p])),
            pl.BlockSpec((4, A_DQK), lambda b, h, p, qi, ki: (0, 0)),
            pl.BlockSpec((1, A_DV), lambda b, h, p, qi, ki: (0, 0)),
        ],
        out_specs=pl.BlockSpec((tq, A_DV), lambda b, h, p, qi, ki: (b * nq + qi[p], h)),
        scratch_shapes=[pltpu.VMEM((2 * tq, A_DV), BF16),
                        pltpu.VMEM((SUBLANE, 2 * tq), F32),
                        pltpu.VMEM((SUBLANE, 2 * tq), F32),
                        pltpu.VMEM((A_DV, 2 * tq), F32),
                        pltpu.VMEM((2, tq, tq), F32)],
    )
    return pl.pallas_call(
        functools.partial(_attn_prompt_kernel, tq=tq, kt=kt, lam_init=lam_init),
        out_shape=jax.ShapeDtypeStruct((mp, A_W), BF16),
        grid_spec=grid_spec,
        compiler_params=_params("arbitrary", "arbitrary", "arbitrary"),
        name=name,
    )(qi_of, ki_of, rel_bias_flat, u_all, u_all, vt, a_lambda, subln_g)


def _attn_sample_kernel(pt_ref, rb_ref, q_ref, kn_ref, vn_ref, lp_ref, g_ref, *rest,
                        n_pg, t_len, page, past, lam_init):
    kp = rest[0:n_pg]
    vp = rest[n_pg:2 * n_pg]
    o_ref = rest[2 * n_pg]
    qs, m_scr, l_scr, acc, knp, vnp = rest[2 * n_pg + 1:]
    s_id = pl.program_id(1)
    n_steps = pl.num_programs(1)
    n_cache = n_steps - 1
    rows = A_HEADS * 2 * t_len
    cols = page * A_HEADS
    row = lax.broadcasted_iota(jnp.int32, (rows, 1), 0)
    row_h = row // (2 * t_len)
    row_q = row % t_len

    def head_col(b):
        c = jnp.full((rows, 1), rb_ref[b], F32)
        for hh in range(1, A_HEADS):
            c = jnp.where(row_h == hh, rb_ref[hh * NUM_BUCKETS + b], c)
        return c

    def exact_bias(dist):
        bucket = _t5_bucket(dist)
        far = head_col(NUM_BUCKETS - 1)
        val = jnp.zeros(dist.shape, F32)
        for b in range(NUM_BUCKETS - 2, -1, -1):
            val = jnp.where(bucket == b, (head_col(b) - far) * LOG2E, val)
        return val

    @pl.when(s_id == 0)
    def _():
        q = q_ref[...] * ((A_DQK ** -0.5) * LOG2E)
        lane = lax.broadcasted_iota(jnp.int32, (t_len, A_DV), 1)
        for hh in range(A_HEADS):
            qh = q[:, hh * A_DV:(hh + 1) * A_DV]
            qs[hh * 2 * t_len:hh * 2 * t_len + t_len, :] = jnp.where(lane < A_DQK, qh, 0.0).astype(BF16)
            qs[hh * 2 * t_len + t_len:(hh + 1) * 2 * t_len, :] = jnp.where(lane >= A_DQK, qh, 0.0).astype(BF16)
        m_scr[...] = jnp.full(m_scr.shape, NEG, F32)
        l_scr[...] = jnp.zeros(l_scr.shape, F32)
        acc[...] = jnp.zeros(acc.shape, F32)

    def head_rows(refs, hh):
        return jnp.concatenate([r[pl.ds(hh, page, stride=A_HEADS), :].astype(BF16) for r in refs], axis=0)

    def cache_step(exact):
        n_keys = n_pg * page
        s_all = jnp.concatenate(
            [_dot_nt(qs[hh * 2 * t_len:(hh + 1) * 2 * t_len, :], head_rows(kp, hh)) for hh in range(A_HEADS)],
            axis=0)
        if exact:
            k_pos = (s_id - 1) * n_keys + lax.broadcasted_iota(jnp.int32, (rows, n_keys), 1)
            s_all = s_all + exact_bias(past + row_q - k_pos)

        def pv(p):
            return jnp.concatenate(
                [_dot(p[hh * 2 * t_len:(hh + 1) * 2 * t_len, :].astype(BF16), head_rows(vp, hh))
                 for hh in range(A_HEADS)], axis=0)

        _softmax_step(s_all, m_scr, l_scr, acc, pv)

    @pl.when((s_id >= 1) & (s_id < n_cache))
    def _():
        cache_step(False)

    @pl.when(s_id == n_cache)
    def _():
        cache_step(True)

    @pl.when(s_id == 0)
    def _():
        knp[...] = jnp.zeros(knp.shape, BF16)
        vnp[...] = jnp.zeros(vnp.shape, BF16)
        kn = kn_ref[...]
        vn = vn_ref[...]
        for hh in range(A_HEADS):
            knp[hh, 0:t_len, :] = kn[:, hh * A_DV:(hh + 1) * A_DV].astype(BF16)
            vnp[hh, 0:t_len, :] = vn[:, hh * A_DV:(hh + 1) * A_DV].astype(BF16)
        kcol = lax.broadcasted_iota(jnp.int32, (rows, page), 1)
        dist = row_q - kcol
        bias = jnp.where((dist >= 0) & (kcol < t_len), exact_bias(dist), NEG)
        parts = []
        for hh in range(A_HEADS):
            parts.append(_dot_nt(qs[hh * 2 * t_len:(hh + 1) * 2 * t_len, :], knp[hh]))
        s = jnp.concatenate(parts, axis=0) + bias

        def pv(p):
            outs = []
            for hh in range(A_HEADS):
                outs.append(_dot(p[hh * 2 * t_len:(hh + 1) * 2 * t_len, :].astype(BF16), vnp[hh]))
            return jnp.concatenate(outs, axis=0)

        _softmax_step(s, m_scr, l_scr, acc, pv)

    @pl.when(s_id == n_cache)
    def _():
        o = acc[...] / l_scr[...]
        lam = _diff_lambda(lp_ref[...], lam_init)
        g = g_ref[...]
        for hh in range(A_HEADS):
            r0 = hh * 2 * t_len
            a = o[r0:r0 + t_len, :] - lam * o[r0 + t_len:r0 + 2 * t_len, :]
            o_ref[:, hh * A_DV:(hh + 1) * A_DV] = (_rms(a, g) * (1.0 - lam_init)).astype(o_ref.dtype)


def _attn_sample(u_all, row0, nb, t_len, cache_k, cache_v, page_table, rel_bias_flat, a_lambda,
                 subln_g, layer, name):
    depth, n_pool, page = cache_k.shape[0], cache_k.shape[1], cache_k.shape[2]
    n_pages = page_table.shape[1]
    n_pg = _tile(n_pages, 16, 1)
    n_cache = n_pages // n_pg
    past = n_pages * page
    rb0 = row0 // t_len
    rows = A_HEADS * 2 * t_len
    ck = cache_k.reshape(depth, n_pool, page * A_HEADS, A_DV)
    cv = cache_v.reshape(depth, n_pool, page * A_HEADS, A_DV)
    pt = page_table.reshape(-1)

    def page_spec(i):
        def imap(b, s, pt_ref):
            step = jnp.maximum(s - 1, 0)
            return (layer, pt_ref[b * n_pages + step * n_pg + i], 0, 0)
        return pl.BlockSpec((None, None, page * A_HEADS, A_DV), imap)

    grid_spec = pltpu.PrefetchScalarGridSpec(
        num_scalar_prefetch=1,
        grid=(nb, n_cache + 1),
        in_specs=[
            pl.BlockSpec(memory_space=pltpu.SMEM),
            pl.BlockSpec((t_len, A_W), lambda b, s, pt_ref: (rb0 + b, C_AQ // A_W)),
            pl.BlockSpec((t_len, A_W), lambda b, s, pt_ref: (rb0 + b, C_AK // A_W)),
            pl.BlockSpec((t_len, A_W), lambda b, s, pt_ref: (rb0 + b, C_AV // A_W)),
            pl.BlockSpec((4, A_DQK), lambda b, s, pt_ref: (0, 0)),
            pl.BlockSpec((1, A_DV), lambda b, s, pt_ref: (0, 0)),
        ] + [page_spec(i) for i in range(n_pg)] + [page_spec(i) for i in range(n_pg)],
        out_specs=pl.BlockSpec((t_len, A_W), lambda b, s, pt_ref: (b, 0)),
        scratch_shapes=[pltpu.VMEM((rows, A_DV), BF16),
                        pltpu.VMEM((rows, 1), F32),
                        pltpu.VMEM((rows, 1), F32),
                        pltpu.VMEM((rows, A_DV), F32),
                        pltpu.VMEM((A_HEADS, page, A_DV), BF16),
                        pltpu.VMEM((A_HEADS, page, A_DV), BF16)],
    )
    return pl.pallas_call(
        functools.partial(_attn_sample_kernel, n_pg=n_pg, t_len=t_len, page=page, past=past,
                          lam_init=_lambda_init(layer)),
        out_shape=jax.ShapeDtypeStruct((nb * t_len, A_W), BF16),
        grid_spec=grid_spec,
        compiler_params=_params("arbitrary", "arbitrary"),
        name=name,
    )(pt, rel_bias_flat, u_all, u_all, u_all, a_lambda, subln_g, *([ck] * n_pg), *([cv] * n_pg))


def _merge_kernel(x_ref, og_ref, oa_ref, ol_ref, gt_ref, wg_ref, wa_ref, wl_ref, wo_ref, o_ref):
    d = x_ref.shape[1]
    gt = gt_ref[...].astype(F32)
    m = gt[:, 0:d] * _dot(og_ref[...], wg_ref[...])
    m = m + gt[:, d:2 * d] * _dot(oa_ref[...], wa_ref[...])
    m = m + gt[:, 2 * d:3 * d] * _dot(ol_ref[...], wl_ref[...])
    o_ref[...] = x_ref[...] + _dot(m.astype(BF16), wo_ref[...])


def _merge(x, o_g, o_a, o_l, gates, wg, wa, wl, wo, name):
    m, d = x.shape
    tm = _tile(m, 640)
    row = lambda w: pl.BlockSpec((tm, w), lambda i: (i, 0))
    full = lambda a: pl.BlockSpec(a.shape, lambda i: (0, 0))
    return pl.pallas_call(
        _merge_kernel,
        out_shape=jax.ShapeDtypeStruct((m, d), F32),
        grid=(m // tm,),
        in_specs=[row(d), row(G_W), row(A_W), row(L_WIDTH), row(3 * d),
                  full(wg), full(wa), full(wl), full(wo)],
        out_specs=row(d),
        compiler_params=_params("parallel"),
        name=name,
    )(x, o_g, o_a, o_l, gates, wg, wa, wl, wo)


def _ffn_kernel(x_ref, g_ref, w1_ref, w3_ref, w2_ref, o_ref, h_scr, acc):
    f = pl.program_id(1)

    @pl.when(f == 0)
    def _():
        x = x_ref[...]
        h_scr[...] = _rms(x, g_ref[...]).astype(BF16)
        acc[...] = x

    h = h_scr[...]
    t = _silu(_dot(h, w1_ref[...])) * _dot(h, w3_ref[...])
    acc[...] += _dot(t.astype(BF16), w2_ref[...])

    @pl.when(f == pl.num_programs(1) - 1)
    def _():
        o_ref[...] = acc[...]


def _ffn(x, g, w1, w3, w2, name):
    m, d = x.shape
    ff = w1.shape[1]
    tm = _tile(m, 1280)
    tf = _tile(ff, 512, LANE)
    return pl.pallas_call(
        _ffn_kernel,
        out_shape=jax.ShapeDtypeStruct((m, d), F32),
        grid=(m // tm, ff // tf),
        in_specs=[
            pl.BlockSpec((tm, d), lambda i, f: (i, 0)),
            pl.BlockSpec((1, d), lambda i, f: (0, 0)),
            pl.BlockSpec((d, tf), lambda i, f: (0, f)),
            pl.BlockSpec((d, tf), lambda i, f: (0, f)),
            pl.BlockSpec((tf, d), lambda i, f: (f, 0)),
        ],
        out_specs=pl.BlockSpec((tm, d), lambda i, f: (i, 0)),
        scratch_shapes=[pltpu.VMEM((tm, d), BF16), pltpu.VMEM((tm, d), F32)],
        compiler_params=_params("parallel", "arbitrary"),
        name=name,
    )(x, g, w1, w3, w2)


def _router_kernel(x_ref, g_ref, wr_ref, h_ref, o_ref):
    h = _rms(x_ref[...], g_ref[...])
    h_ref[...] = h.astype(BF16)
    wr_hi, wr_lo = _split(wr_ref[...])
    logits = _dot_split(h, wr_hi, wr_lo)
    lane = lax.broadcasted_iota(jnp.int32, logits.shape, 1)
    logits = jnp.where(lane < N_EXPERTS, logits, -jnp.inf)
    m1 = jnp.max(logits, axis=-1, keepdims=True)
    i1 = jnp.min(jnp.where(logits == m1, lane, LANE), axis=-1, keepdims=True)
    rest = jnp.where(lane == i1, -jnp.inf, logits)
    m2 = jnp.max(rest, axis=-1, keepdims=True)
    i2 = jnp.min(jnp.where(rest == m2, lane, LANE), axis=-1, keepdims=True)
    e2 = jnp.exp(m2 - m1)
    den = 1.0 + e2
    o_ref[...] = jnp.where(lane == 0, i1.astype(F32),
                           jnp.where(lane == 1, i2.astype(F32),
                                     jnp.where(lane == 2, 1.0 / den, jnp.where(lane == 3, e2 / den, 0.0))))


def _router(x, g, wr, name):
    m, d = x.shape
    tm = _tile(m, 1280)
    return pl.pallas_call(
        _router_kernel,
        out_shape=(jax.ShapeDtypeStruct((m, d), BF16), jax.ShapeDtypeStruct((m, LANE), F32)),
        grid=(m // tm,),
        in_specs=[pl.BlockSpec((tm, d), lambda i: (i, 0)),
                  pl.BlockSpec((1, d), lambda i: (0, 0)),
                  pl.BlockSpec((d, LANE), lambda i: (0, 0))],
        out_specs=(pl.BlockSpec((tm, d), lambda i: (i, 0)), pl.BlockSpec((tm, LANE), lambda i: (i, 0))),
        compiler_params=_params("parallel"),
        name=name,
    )(x, g, wr)


MOE_ROW_BLOCK = 640


def _expert_ffn_kernel(be_ref, na_ref, x_ref, w1_ref, w3_ref, w2_ref, o_ref, acc):
    b = pl.program_id(0)
    f = pl.program_id(1)
    active = b < na_ref[0]

    def partial_out():
        x = x_ref[...]
        t = _silu(_dot(x, w1_ref[...])) * _dot(x, w3_ref[...])
        return _dot(t.astype(BF16), w2_ref[...])

    @pl.when(active & (f == 0))
    def _():
        acc[...] = partial_out()

    @pl.when(active & (f > 0))
    def _():
        acc[...] += partial_out()

    @pl.when(jnp.logical_not(active) & (f == 0))
    def _():
        acc[...] = jnp.zeros(acc.shape, F32)

    @pl.when(f == pl.num_programs(1) - 1)
    def _():
        o_ref[...] = acc[...].astype(o_ref.dtype)


def _expert_ffn(xs, block_expert, n_active, w1, w3, w2, name):
    n, d = xs.shape
    ff = w1.shape[2]
    rb = MOE_ROW_BLOCK
    tf = _tile(ff, 1792, LANE)
    grid_spec = pltpu.PrefetchScalarGridSpec(
        num_scalar_prefetch=2,
        grid=(n // rb, ff // tf),
        in_specs=[
            pl.BlockSpec((rb, d), lambda b, f, be, na: (b, 0)),
            pl.BlockSpec((None, d, tf), lambda b, f, be, na: (be[b], 0, f)),
            pl.BlockSpec((None, d, tf), lambda b, f, be, na: (be[b], 0, f)),
            pl.BlockSpec((None, tf, d), lambda b, f, be, na: (be[b], f, 0)),
        ],
        out_specs=pl.BlockSpec((rb, d), lambda b, f, be, na: (b, 0)),
        scratch_shapes=[pltpu.VMEM((rb, d), F32)],
    )
    return pl.pallas_call(
        _expert_ffn_kernel,
        out_shape=jax.ShapeDtypeStruct((n, d), BF16),
        grid_spec=grid_spec,
        compiler_params=_params("arbitrary", "arbitrary"),
        name=name,
    )(block_expert, n_active, xs, w1, w3, w2)


def _combine_kernel(x_ref, y1_ref, y2_ref, r_ref, fg_ref, o_ref):
    r = r_ref[...]
    lane = lax.broadcasted_iota(jnp.int32, r.shape, 1)
    w1 = jnp.sum(jnp.where(lane == 2, r, 0.0), axis=-1, keepdims=True)
    w2 = jnp.sum(jnp.where(lane == 3, r, 0.0), axis=-1, keepdims=True)
    y = w1 * y1_ref[...].astype(F32) + w2 * y2_ref[...].astype(F32)
    o_ref[...] = _rms(x_ref[...] + y, fg_ref[...])


def _combine(x, y1, y2, route, final_g, name):
    m, d = x.shape
    tm = _tile(m, 640)
    row = lambda w: pl.BlockSpec((tm, w), lambda i: (i, 0))
    return pl.pallas_call(
        _combine_kernel,
        out_shape=jax.ShapeDtypeStruct((m, d), F32),
        grid=(m // tm,),
        in_specs=[row(d), row(d), row(d), row(LANE), pl.BlockSpec((1, d), lambda i: (0, 0))],
        out_specs=row(d),
        compiler_params=_params("parallel"),
        name=name,
    )(x, y1, y2, route, final_g)


def _moe_routed(x, g, wr, w1, w3, w2, final_g, layer):
    m, d = x.shape
    ne = w1.shape[0]
    rb = MOE_ROW_BLOCK
    h16, route = _router(x, g, wr, f"router_{layer}")
    eid = route[:, 0:TOP_K].astype(jnp.int32).reshape(-1)
    n_assign = TOP_K * m
    onehot = (eid[:, None] == jnp.arange(ne, dtype=jnp.int32)[None, :]).astype(jnp.int32)
    csum = jnp.cumsum(onehot, axis=0)
    counts = csum[n_assign - 1]
    starts = jnp.cumsum(counts) - counts
    padded = (counts + rb - 1) // rb * rb
    pad_end = jnp.cumsum(padded)
    pad_start = pad_end - padded
    rank = jnp.sum(onehot * csum, axis=1) - 1
    pos = pad_start[eid] + rank
    n_rows = (n_assign + rb - 1) // rb * rb + ne * rb
    block_start = jnp.arange(n_rows // rb, dtype=jnp.int32) * rb
    block_expert = jnp.minimum(jnp.searchsorted(pad_end, block_start, side="right"), ne - 1).astype(jnp.int32)
    n_active = (pad_end[ne - 1:ne] // rb).astype(jnp.int32)
    order = jnp.argsort(eid, stable=True)
    row_e = jnp.repeat(block_expert, rb)
    row_r = jnp.arange(n_rows, dtype=jnp.int32) - pad_start[row_e]
    src = order[jnp.minimum(starts[row_e] + row_r, n_assign - 1)] // TOP_K
    src_tok = jnp.where(row_r < counts[row_e], src, 0)
    xs = jnp.take(h16, src_tok, axis=0)
    ys = _expert_ffn(xs, block_expert, n_active, w1, w3, w2, f"moe_{layer}")
    pos = pos.reshape(m, TOP_K)
    y1 = jnp.take(ys, pos[:, 0], axis=0)
    y2 = jnp.take(ys, pos[:, 1], axis=0)
    return _combine(x, y1, y2, route, final_g, f"moe_combine_{layer}")


def _final_norm_kernel(x_ref, g_ref, o_ref):
    o_ref[...] = _rms(x_ref[...], g_ref[...])


def _final_norm(x, g, name):
    m, d = x.shape
    tm = _tile(m, 1280)
    return pl.pallas_call(
        _final_norm_kernel,
        out_shape=jax.ShapeDtypeStruct((m, d), F32),
        grid=(m // tm,),
        in_specs=[pl.BlockSpec((tm, d), lambda i: (i, 0)), pl.BlockSpec((1, d), lambda i: (0, 0))],
        out_specs=pl.BlockSpec((tm, d), lambda i: (i, 0)),
        compiler_params=_params("parallel"),
        name=name,
    )(x, g)


def _split_w_in(w):
    o_gb = 4 * G_W
    o_aq = o_gb + 2 * G_HEADS
    pad = jnp.zeros((w.shape[0], LANE - 2 * G_HEADS), w.dtype)
    return (jnp.concatenate([w[:, :o_gb], w[:, o_aq:]], axis=1),
            jnp.concatenate([w[:, o_gb:o_aq], pad], axis=1))


def _block_diag(w):
    n, d, _ = w.shape
    eye = jnp.eye(n, dtype=w.dtype)
    return (eye[:, None, :, None] * w[:, :, None, :]).reshape(n * d, n * d)


def kernel(x_prompt, x_sample, cache_k, cache_v, page_table, state_gdn, state_gdn_conv, state_lru, state_lru_conv, norm1_g, norm2_g, final_g, w_in, w_gate, b_gate, g_conv_w, g_a_log, g_dt_bias, g_norm_g, a_lambda, a_subln_g, rel_bias, l_conv_w, l_conv_b, l_wa, l_ba, l_wx, l_bx, l_lambda, w_br_g, w_br_a, w_br_l, w_out, f_w1, f_w3, f_w2, m_router, m_w1, m_w3, m_w2):
    bp, tp, d = x_prompt.shape
    bs, ts, _ = x_sample.shape
    depth = w_in.shape[0]
    mp = bp * tp
    ms = bs * ts
    x = jnp.concatenate([x_prompt.reshape(mp, d), x_sample.reshape(ms, d)], axis=0)
    rb_flat = rel_bias.T.reshape(-1)
    row = lambda v: v.reshape(1, -1)

    k_p, v_p, k_s, v_s = [], [], [], []
    sg_p, sg_s, cg_p, cg_s, sl_p, sl_s, cl_p, cl_s = [], [], [], [], [], [], [], []
    y = None
    for l in range(depth):
        w_main, w_gba = _split_w_in(w_in[l])
        u = _norm_matmul(x, row(norm1_g[l]), w_main.astype(BF16), jnp.zeros((1, U_COLS), F32),
                         None, F32, f"in_proj_{l}")
        gba = _norm_matmul(x, row(norm1_g[l]), w_gba.astype(BF16), jnp.zeros((1, LANE), F32),
                           None, F32, f"in_proj_gba_{l}")
        gates = _norm_matmul(x, row(norm1_g[l]), w_gate[l].astype(BF16), row(b_gate[l]),
                             "sigmoid", BF16, f"gate_proj_{l}")

        ab = jnp.zeros((2, LANE), F32)
        ab = ab.at[0, G_HEADS:2 * G_HEADS].set(g_a_log[l]).at[1, G_HEADS:2 * G_HEADS].set(g_dt_bias[l])
        ng = row(g_norm_g[l])
        og_p, s_p = _gdn(u, gba, 0, bp, tp, jnp.zeros((bp, CONV_W - 1, 3 * G_W), F32),
                         jnp.zeros((bp, G_HEADS, G_DK, G_DV), F32), g_conv_w[l], ab, ng, f"gdn_prompt_{l}")
        og_s, s_s = _gdn(u, gba, mp, bs, ts, state_gdn_conv[l], state_gdn[l], g_conv_w[l], ab, ng,
                         f"gdn_sample_{l}")

        sub_g = row(a_subln_g[l])
        oa_p = _attn_prompt(u, bp, tp, rb_flat, a_lambda[l], sub_g, l, f"attn_prompt_{l}")
        oa_s = _attn_sample(u, mp, bs, ts, cache_k, cache_v, page_table, rb_flat, a_lambda[l], sub_g, l,
                            f"attn_sample_{l}")

        wa_bd = _block_diag(l_wa[l]).astype(BF16)
        wx_bd = _block_diag(l_wx[l]).astype(BF16)
        lru_args = (l_conv_w[l], row(l_conv_b[l]), wa_bd, row(l_ba[l]), wx_bd, row(l_bx[l]), row(l_lambda[l]))
        ol_p, h_p = _lru(u, 0, bp, tp, jnp.zeros((bp, CONV_W - 1, L_WIDTH), F32),
                         jnp.zeros((bp, L_WIDTH), F32), *lru_args, f"lru_prompt_{l}")
        ol_s, h_s = _lru(u, mp, bs, ts, state_lru_conv[l], state_lru[l], *lru_args, f"lru_sample_{l}")

        x = _merge(x, jnp.concatenate([og_p, og_s]), jnp.concatenate([oa_p, oa_s]),
                   jnp.concatenate([ol_p, ol_s]), gates, w_br_g[l].astype(BF16), w_br_a[l].astype(BF16),
                   w_br_l[l].astype(BF16), w_out[l].astype(BF16), f"merge_{l}")

        j = l // 2
        if l % 2 == 0:
            x = _ffn(x, row(norm2_g[l]), f_w1[j].astype(BF16), f_w3[j].astype(BF16), f_w2[j].astype(BF16),
                     f"ffn_{l}")
            if l == depth - 1:
                y = _final_norm(x, row(final_g), "final_norm")
        else:
            wr = jnp.zeros((d, LANE), F32).at[:, :N_EXPERTS].set(m_router[j])
            if l != depth - 1:
                raise NotImplementedError("mixture layer is only fused with the final norm")
            y = _moe_routed(x, row(norm2_g[l]), wr, m_w1[j].astype(BF16), m_w3[j].astype(BF16),
                            m_w2[j].astype(BF16), row(final_g), l)

        def tail_rows(c0, width):
            keep = CONV_W - 1
            p = jnp.stack([u[(b + 1) * tp - keep:(b + 1) * tp, c0:c0 + width] for b in range(bp)])
            s = u[mp:, c0:c0 + width].reshape(bs, ts, width)[:, ts - keep:]
            return p, s

        k_p.append(u[:mp, C_AK:C_AK + A_W].reshape(bp, tp, A_HEADS, A_DV))
        v_p.append(u[:mp, C_AV:C_AV + A_W].reshape(bp, tp, A_HEADS, A_DV))
        k_s.append(u[mp:, C_AK:C_AK + A_W].reshape(bs, ts, A_HEADS, A_DV))
        v_s.append(u[mp:, C_AV:C_AV + A_W].reshape(bs, ts, A_HEADS, A_DV))
        sg_p.append(s_p)
        sg_s.append(s_s)
        cp, cs = tail_rows(C_GQKV, 3 * G_W)
        cg_p.append(cp)
        cg_s.append(cs)
        sl_p.append(h_p)
        sl_s.append(h_s)
        cp, cs = tail_rows(C_LX, L_WIDTH)
        cl_p.append(cp)
        cl_s.append(cs)

    st = jnp.stack
    return (y[:mp].reshape(bp, tp, d), y[mp:].reshape(bs, ts, d),
            st(k_p), st(v_p), st(k_s), st(v_s), st(sg_p), st(sg_s), st(cg_p), st(cg_s),
            st(sl_p), st(sl_s), st(cl_p), st(cl_s))
```

```python
import functools
import math

import jax
import jax.numpy as jnp
from jax import lax
from jax.experimental import pallas as pl
from jax.experimental.pallas import tpu as pltpu

F32 = jnp.float32
BF16 = jnp.bfloat16
EPS = 1e-6
NEG = -1e30

G_HEADS, G_DK, G_DV = 4, 128, 128
G_W = G_HEADS * G_DK
GDN_CHUNK = 64
GDN_SEQS_PER_STEP = 2
CONV_W = 4
A_HEADS, A_DQK, A_DV = 4, 64, 128
A_W = A_HEADS * A_DV
L_WIDTH, L_BLOCKS = 512, 8
LRU_C = 8.0
NUM_BUCKETS, MAX_DISTANCE = 32, 128
N_EXPERTS, TOP_K = 8, 2
LANE = 128
SUBLANE = 8
VMEM_LIMIT = 56 * 1024 * 1024

C_GQKV = 0
C_GZ = 3 * G_W
C_AQ = 4 * G_W
C_AK = C_AQ + A_W
C_AV = C_AK + A_W
C_LX = C_AV + A_W
C_LY = C_LX + L_WIDTH
U_COLS = C_LY + L_WIDTH


def _tile(n, target, mult=SUBLANE):
    best = None
    for d in range(mult, min(n, target) + 1, mult):
        if n % d == 0:
            best = d
    return best if best is not None else n


def _params(*sem):
    return pltpu.CompilerParams(dimension_semantics=sem, vmem_limit_bytes=VMEM_LIMIT)


def _dot(a, b):
    return jnp.dot(a, b, preferred_element_type=F32)


def _dot_nt(a, b):
    return lax.dot_general(a, b, (((1,), (1,)), ((), ())), preferred_element_type=F32)


def _dot_tn(a, b):
    return lax.dot_general(a, b, (((0,), (0,)), ((), ())), preferred_element_type=F32)


def _split(x):
    hi = x.astype(BF16)
    return hi, (x - hi.astype(F32)).astype(BF16)


def _dot_split(a, b_hi, b_lo):
    a_hi, a_lo = _split(a)
    return _dot(a_hi, b_hi) + (_dot(a_hi, b_lo) + _dot(a_lo, b_hi))


def _dot_lhs_exact(a16, b):
    b_hi, b_lo = _split(b)
    return _dot(a16, b_hi) + _dot(a16, b_lo)


def _sigmoid(x):
    return 1.0 / (1.0 + jnp.exp(-x))


def _silu(x):
    return x * _sigmoid(x)


def _softplus(x):
    return jnp.maximum(x, 0.0) + jnp.log1p(jnp.exp(-jnp.abs(x)))


def _rms(x, g):
    return x * lax.rsqrt(jnp.mean(x * x, axis=-1, keepdims=True) + EPS) * g


def _norm_mm_kernel(x_ref, g_ref, w_ref, b_ref, o_ref, h_scr, *, act):
    @pl.when(pl.program_id(1) == 0)
    def _():
        h_scr[...] = _rms(x_ref[...], g_ref[...]).astype(BF16)

    acc = _dot(h_scr[...], w_ref[...]) + b_ref[...]
    if act == "sigmoid":
        acc = _sigmoid(acc)
    o_ref[...] = acc.astype(o_ref.dtype)


def _norm_matmul(x, g, w, b, act, out_dtype, name):
    m, k = x.shape
    n = w.shape[1]
    tm = _tile(m, 1280)
    tn = _tile(n, 1536, LANE)
    return pl.pallas_call(
        functools.partial(_norm_mm_kernel, act=act),
        out_shape=jax.ShapeDtypeStruct((m, n), out_dtype),
        grid=(m // tm, n // tn),
        in_specs=[
            pl.BlockSpec((tm, k), lambda i, j: (i, 0)),
            pl.BlockSpec((1, k), lambda i, j: (0, 0)),
            pl.BlockSpec((k, tn), lambda i, j: (0, j)),
            pl.BlockSpec((1, tn), lambda i, j: (0, j)),
        ],
        out_specs=pl.BlockSpec((tm, tn), lambda i, j: (i, j)),
        scratch_shapes=[pltpu.VMEM((tm, k), BF16)],
        compiler_params=_params("parallel", "arbitrary"),
        name=name,
    )(x, g, w, b)


def _gdn_kernel(*refs, ns, tt, chunk, rows):
    seq_in = [refs[3 * i:3 * i + 3] for i in range(ns)]
    cs_ref, s0_ref, cw_ref, ab_ref, ng_ref, o_ref, sout_ref, xe, ys, bg, s_scr = refs[3 * ns:]
    t = pl.program_id(1)
    nt = pl.num_programs(1)
    hc = G_HEADS * chunk

    @pl.when(t == 0)
    def _():
        for s in range(ns):
            for h in range(G_HEADS):
                s_scr[s, :, h * G_DV:(h + 1) * G_DV] = s0_ref[s, h]
            xe[s] = jnp.zeros(xe.shape[1:], F32)
            xe[s, SUBLANE - (CONV_W - 1):SUBLANE, :] = cs_ref[s]

    cw = cw_ref[...]
    ab = ab_ref[...]
    base = SUBLANE - (CONV_W - 1)
    for s in range(ns):
        qkv_ref, _, gba_ref = seq_in[s]
        xe[s, SUBLANE:SUBLANE + tt, :] = qkv_ref[...]
        y = xe[s, base:base + rows, :] * cw[0:1, :]
        for j in range(1, CONV_W):
            y = y + xe[s, base + j:base + j + rows, :] * cw[j:j + 1, :]
        xe[s, 0:SUBLANE, :] = xe[s, tt:tt + SUBLANE, :]

        gba = gba_ref[...]
        if rows > tt:
            gba = jnp.concatenate([gba, jnp.zeros((rows - tt, LANE), F32)], axis=0)
        lane_g = lax.broadcasted_iota(jnp.int32, (rows, LANE), 1)
        bgc = jnp.where(lane_g < G_HEADS, _sigmoid(gba), -jnp.exp(ab[0:1, :]) * _softplus(gba + ab[1:2, :]))
        y = _silu(y)
        if rows > tt:
            live = lax.broadcasted_iota(jnp.int32, (rows, 1), 0) < tt
            y = jnp.where(live, y, 0.0)
            bgc = jnp.where(live, bgc, 0.0)
        bg[s] = bgc
        for h in range(G_HEADS):
            q = y[:, h * G_DK:(h + 1) * G_DK]
            k = y[:, G_W + h * G_DK:G_W + (h + 1) * G_DK]
            ys[s, :, h * G_DK:(h + 1) * G_DK] = (
                q * lax.rsqrt(jnp.sum(q * q, axis=-1, keepdims=True) + 1e-6) * (G_DK ** -0.5))
            ys[s, :, G_W + h * G_DK:G_W + (h + 1) * G_DK] = (
                k * lax.rsqrt(jnp.sum(k * k, axis=-1, keepdims=True) + 1e-6))
        ys[s, :, 2 * G_W:3 * G_W] = y[:, 2 * G_W:3 * G_W]

    def iota(shape, d):
        return lax.broadcasted_iota(jnp.int32, shape, d)

    r_c = iota((chunk, hc), 0)
    j_c = iota((chunk, hc), 1) % chunk
    tril_cat = r_c >= j_c
    strict_cat = r_c > j_c
    eye_cat = (r_c == j_c).astype(F32)
    triu_cat = (r_c <= j_c).astype(F32)
    tril_f = jnp.where(iota((chunk, chunk), 0) >= iota((chunk, chunk), 1), 1.0, 0.0).astype(BF16)
    ones_cc = jnp.ones((chunk, chunk), BF16)
    l_s = iota((LANE, hc + 2 * G_W), 0)
    n_s = iota((LANE, hc + 2 * G_W), 1)
    src_lane = jnp.where(n_s < hc, G_HEADS + n_s // chunk,
                         jnp.where(n_s < hc + G_W, G_HEADS + (n_s - hc) // G_DK,
                                   (n_s - hc - G_W) // G_DK))
    sel = jnp.where(l_s == src_lane, 1.0, 0.0).astype(BF16)
    bd_cc = iota((hc, hc), 0) // chunk == iota((hc, hc), 1) // chunk
    bd_cd = iota((hc, G_W), 0) // chunk == iota((hc, G_W), 1) // G_DK
    bd_dd = iota((G_W, G_W), 0) // G_DK == iota((G_W, G_W), 1) // G_DK
    ng = ng_ref[...]
    n_doubling = int(math.log2(chunk)) - 1

    def blockdiag(x, mask):
        reps = mask.shape[0] // x.shape[0]
        return jnp.where(mask, jnp.concatenate([x] * reps, axis=0), 0.0)

    seqs = list(range(ns))

    def each(f, *cols):
        return [f(*[c[s] for c in cols]) for s in seqs]

    def phase_a(c0):
        bgs = each(lambda s: _split(bg[s, c0:c0 + chunk, :]), seqs)
        selm = each(lambda b: _dot(b[0], sel) + _dot(b[1], sel), bgs)
        cum = each(lambda m: _dot_lhs_exact(tril_f, m[:, 0:hc + G_W]), selm)
        gj_cat = each(lambda m: _dot_lhs_exact(ones_cc, m[:, 0:hc] * triu_cat), selm)
        gi = each(lambda c: c[:, hc:hc + G_W], cum)
        beta = each(lambda m: m[:, hc + G_W:hc + 2 * G_W], selm)
        decay = each(lambda c, gj: jnp.where(tril_cat, jnp.exp(c[:, 0:hc] - gj), 0.0), cum, gj_cat)
        egi = each(jnp.exp, gi)
        q = each(lambda s: ys[s, c0:c0 + chunk, 0:G_W], seqs)
        k = each(lambda s: ys[s, c0:c0 + chunk, G_W:2 * G_W], seqs)
        v = each(lambda s: ys[s, c0:c0 + chunk, 2 * G_W:3 * G_W], seqs)
        kb = each(lambda a, b: a * b, k, beta)
        bdk = each(lambda a: blockdiag(a, bd_cd).astype(BF16), k)
        la = each(lambda a, b, c: _dot_nt(jnp.concatenate([a, b], axis=0).astype(BF16), c), kb, q, bdk)
        a_in = each(lambda l, d: (l[chunk:2 * chunk] * d).astype(BF16), la, decay)
        pw = each(lambda l, d: -jnp.where(strict_cat, l[0:chunk] * d, 0.0), la, decay)
        tinv = each(lambda p: eye_cat + p, pw)
        bd = each(lambda p: _split(blockdiag(p, bd_cc)), pw)
        for _ in range(n_doubling):
            pw = each(lambda p, b: _dot_split(p, b[0], b[1]), pw, bd)
            bd = each(lambda p: _split(blockdiag(p, bd_cc)), pw)
            tinv = each(lambda ti, b: ti + _dot_split(ti, b[0], b[1]), tinv, bd)
        rhs = each(lambda vv, b, kk, e: jnp.concatenate(
            [blockdiag(vv * b, bd_cd), blockdiag(kk * e, bd_cd)], axis=1).astype(BF16), v, beta, kb, egi)
        uw = each(lambda ti, r: _dot(ti.astype(BF16), r), tinv, rhs)
        return dict(uw=uw, a=a_in, qe=each(lambda a, e: a * e, q, egi),
                    kd=each(lambda a, g: (a * jnp.exp(g[chunk - 1:chunk, :] - g)).astype(BF16), k, gi),
                    gl=each(lambda g: jnp.exp(g[chunk - 1:chunk, :]), gi))

    def phase_b(c0, a):
        s_cat = each(lambda s: s_scr[s], seqs)
        bds = each(lambda x: blockdiag(x, bd_dd).astype(BF16), s_cat)
        ws = each(lambda uw, qe, b: _dot(jnp.concatenate([uw[:, G_W:2 * G_W], qe], axis=0).astype(BF16), b),
                  a["uw"], a["qe"], bds)
        v_new = each(lambda uw, w: uw[:, 0:G_W] - w[0:chunk], a["uw"], ws)
        bdv = each(lambda x: blockdiag(x, bd_cd).astype(BF16), v_new)
        o = each(lambda w, ai, b: w[chunk:2 * chunk] + _dot(ai, b), ws, a["a"], bdv)
        v16 = each(lambda x: x.astype(BF16), v_new)
        n_out = min(tt, chunk)
        for h in range(G_HEADS):
            hs = slice(h * G_DV, (h + 1) * G_DV)
            for s in seqs:
                s_scr[s, :, hs] = (s_cat[s][:, hs] * a["gl"][s][:, hs]
                                   + _dot_tn(a["kd"][s][:, hs], v16[s][:, hs]))
            for s in seqs:
                zz = seq_in[s][1][c0:c0 + n_out, hs]
                o_ref[s, c0:c0 + n_out, hs] = (_rms(o[s][0:n_out, hs], ng) * _silu(zz)).astype(o_ref.dtype)

    n_chunks = rows // chunk
    nxt = phase_a(0)
    for c in range(n_chunks):
        cur = nxt
        if c + 1 < n_chunks:
            nxt = phase_a((c + 1) * chunk)
        phase_b(c * chunk, cur)

    @pl.when(t == nt - 1)
    def _():
        for s in range(ns):
            for h in range(G_HEADS):
                sout_ref[s, h] = s_scr[s, :, h * G_DV:(h + 1) * G_DV]


def _gdn(u_all, gba_all, row0, nb, t_len, conv_state, s0, conv_w, ab, norm_g, name):
    chunk = GDN_CHUNK
    if t_len >= chunk:
        tt = _tile(t_len, 256, chunk)
        rows = tt
    else:
        tt = t_len
        rows = chunk
    nt = t_len // tt
    rb0 = row0 // tt
    w3 = 3 * G_W
    ns = GDN_SEQS_PER_STEP if nb % GDN_SEQS_PER_STEP == 0 else 1

    def seq_specs(s):
        blk = lambda b, t: rb0 + (b * ns + s) * nt + t
        return [pl.BlockSpec((tt, w3), lambda b, t: (blk(b, t), C_GQKV // w3)),
                pl.BlockSpec((tt, G_W), lambda b, t: (blk(b, t), C_GZ // G_W)),
                pl.BlockSpec((tt, LANE), lambda b, t: (blk(b, t), 0))]

    out, s_new = pl.pallas_call(
        functools.partial(_gdn_kernel, ns=ns, tt=tt, chunk=chunk, rows=rows),
        out_shape=(jax.ShapeDtypeStruct((nb, t_len, G_W), BF16),
                   jax.ShapeDtypeStruct((nb, G_HEADS, G_DK, G_DV), F32)),
        grid=(nb // ns, nt),
        in_specs=[spec for s in range(ns) for spec in seq_specs(s)] + [
            pl.BlockSpec((ns, CONV_W - 1, w3), lambda b, t: (b, 0, 0)),
            pl.BlockSpec((ns, G_HEADS, G_DK, G_DV), lambda b, t: (b, 0, 0, 0)),
            pl.BlockSpec((CONV_W, w3), lambda b, t: (0, 0)),
            pl.BlockSpec((2, LANE), lambda b, t: (0, 0)),
            pl.BlockSpec((1, G_DV), lambda b, t: (0, 0)),
        ],
        out_specs=(pl.BlockSpec((ns, tt, G_W), lambda b, t: (b, t, 0)),
                   pl.BlockSpec((ns, G_HEADS, G_DK, G_DV), lambda b, t: (b, 0, 0, 0))),
        scratch_shapes=[pltpu.VMEM((ns, SUBLANE + rows, w3), F32),
                        pltpu.VMEM((ns, rows, w3), F32),
                        pltpu.VMEM((ns, rows, LANE), F32),
                        pltpu.VMEM((ns, G_DK, G_HEADS * G_DV), F32)],
        compiler_params=_params("parallel", "arbitrary"),
        name=name,
    )(*([u_all, u_all, gba_all] * ns), conv_state, s0, conv_w, ab, norm_g)
    return out.reshape(nb * t_len, G_W), s_new


def _gelu_tanh(x):
    return 0.5 * x * (1.0 + jnp.tanh(math.sqrt(2.0 / math.pi) * (x + 0.044715 * (x * x * x))))


def _lru_kernel(lx_ref, ly_ref, cs_ref, h0_ref, cw_ref, cb_ref, wa_ref, ba_ref, wx_ref, bx_ref,
                lam_ref, o_ref, hout_ref, xe, h_scr, *, tt):
    t = pl.program_id(1)
    nt = pl.num_programs(1)

    @pl.when(t == 0)
    def _():
        h_scr[...] = jnp.broadcast_to(h0_ref[...], h_scr.shape)
        xe[0:SUBLANE, :] = jnp.zeros((SUBLANE, L_WIDTH), F32)
        xe[SUBLANE - (CONV_W - 1):SUBLANE, :] = cs_ref[...]

    xe[SUBLANE:SUBLANE + tt, :] = lx_ref[...]
    cw = cw_ref[...]
    base = SUBLANE - (CONV_W - 1)
    xc = xe[base:base + tt, :] * cw[0:1, :]
    for j in range(1, CONV_W):
        xc = xc + xe[base + j:base + j + tt, :] * cw[j:j + 1, :]
    xe[0:SUBLANE, :] = xe[tt:tt + SUBLANE, :]
    xc = xc + cb_ref[...]
    x16 = xc.astype(BF16)
    r = _sigmoid(_dot(x16, wa_ref[...]) + ba_ref[...])
    i = _sigmoid(_dot(x16, wx_ref[...]) + bx_ref[...])
    log_a = -LRU_C * r * _softplus(-lam_ref[...])
    a = jnp.exp(log_a)
    b = jnp.sqrt(jnp.tanh(-log_a) * (a * a + 1.0)) * (i * xc)
    row = lax.broadcasted_iota(jnp.int32, (tt, 1), 0)
    s = 1
    while s < tt:
        keep = row >= s
        a_prev = jnp.where(keep, pltpu.roll(a, s, 0), 1.0)
        b_prev = jnp.where(keep, pltpu.roll(b, s, 0), 0.0)
        b = a * b_prev + b
        a = a * a_prev
        s *= 2
    h = a * h_scr[0:1, :] + b
    h_last = h[tt - 1:tt, :]
    h_scr[...] = jnp.broadcast_to(h_last, h_scr.shape)
    o_ref[...] = (h * _gelu_tanh(ly_ref[...])).astype(o_ref.dtype)

    @pl.when(t == nt - 1)
    def _():
        hout_ref[...] = h_last


def _lru(u_all, row0, nb, t_len, conv_state, h0, cw, cb, wa, ba, wx, bx, lam, name):
    tt = _tile(t_len, 256)
    nt = t_len // tt
    rb0 = row0 // tt
    vec = lambda: pl.BlockSpec((1, L_WIDTH), lambda b, t: (0, 0))
    out, h_last = pl.pallas_call(
        functools.partial(_lru_kernel, tt=tt),
        out_shape=(jax.ShapeDtypeStruct((nb * t_len, L_WIDTH), BF16),
                   jax.ShapeDtypeStruct((nb, 1, L_WIDTH), F32)),
        grid=(nb, nt),
        in_specs=[
            pl.BlockSpec((tt, L_WIDTH), lambda b, t: (rb0 + b * nt + t, C_LX // L_WIDTH)),
            pl.BlockSpec((tt, L_WIDTH), lambda b, t: (rb0 + b * nt + t, C_LY // L_WIDTH)),
            pl.BlockSpec((None, CONV_W - 1, L_WIDTH), lambda b, t: (b, 0, 0)),
            pl.BlockSpec((None, 1, L_WIDTH), lambda b, t: (b, 0, 0)),
            pl.BlockSpec((CONV_W, L_WIDTH), lambda b, t: (0, 0)),
            vec(),
            pl.BlockSpec((L_WIDTH, L_WIDTH), lambda b, t: (0, 0)),
            vec(),
            pl.BlockSpec((L_WIDTH, L_WIDTH), lambda b, t: (0, 0)),
            vec(),
            vec(),
        ],
        out_specs=(pl.BlockSpec((tt, L_WIDTH), lambda b, t: (b * nt + t, 0)),
                   pl.BlockSpec((None, 1, L_WIDTH), lambda b, t: (b, 0, 0))),
        scratch_shapes=[pltpu.VMEM((SUBLANE + tt, L_WIDTH), F32),
                        pltpu.VMEM((SUBLANE, L_WIDTH), F32)],
        compiler_params=_params("parallel", "arbitrary"),
        name=name,
    )(u_all, u_all, conv_state, h0.reshape(nb, 1, L_WIDTH), cw, cb, wa, ba, wx, bx, lam)
    return out, h_last.reshape(nb, L_WIDTH)


def _t5_bucket(dist):
    n = jnp.maximum(dist, 0)
    max_exact = NUM_BUCKETS // 2
    nf = jnp.maximum(n, 1).astype(F32)
    large = max_exact + (jnp.log(nf / max_exact) / math.log(MAX_DISTANCE / max_exact)
                         * (NUM_BUCKETS - max_exact)).astype(jnp.int32)
    large = jnp.minimum(large, NUM_BUCKETS - 1)
    return jnp.where(n < max_exact, n, large)


def _lambda_init(layer):
    return 0.8 - 0.6 * math.exp(-0.3 * layer)


def _diff_lambda(lp, lam_init):
    s1 = jnp.sum(lp[0:1, :] * lp[1:2, :], axis=-1, keepdims=True)
    s2 = jnp.sum(lp[2:3, :] * lp[3:4, :], axis=-1, keepdims=True)
    return jnp.exp(s1) - jnp.exp(s2) + lam_init


def _softmax_step(s, m_ref, l_ref, acc_ref, pv):
    m_prev = m_ref[...]
    m_new = jnp.maximum(m_prev, jnp.max(s, axis=-1, keepdims=True))
    alpha = jnp.exp2(m_prev - m_new)
    p = jnp.exp2(s - m_new)
    l_ref[...] = alpha * l_ref[...] + jnp.sum(p, axis=-1, keepdims=True)
    acc_ref[...] = alpha * acc_ref[...] + pv(p)
    m_ref[...] = m_new


LOG2E = math.log2(math.e)
ATTN_COL_CHUNK = LANE


def _attn_prompt_kernel(qi_ref, ki_ref, rb_ref, q_ref, k_ref, vt_ref, lp_ref, g_ref, o_ref,
                        qs, m_scr, l_scr, acc, tab, *, tq, kt, lam_init):
    h = pl.program_id(1)
    qi = qi_ref[pl.program_id(2)]
    ki = ki_ref[pl.program_id(2)]
    cw = min(ATTN_COL_CHUNK, tq)

    @pl.when((qi == 0) & (ki == 0))
    def _():
        kj = lax.broadcasted_iota(jnp.int32, (tq, tq), 0)
        qq = lax.broadcasted_iota(jnp.int32, (tq, tq), 1)
        far = rb_ref[h * NUM_BUCKETS + NUM_BUCKETS - 1]
        for which in range(2):
            dist = qq - kj + which * tq
            bucket = _t5_bucket(dist)
            val = jnp.zeros((tq, tq), F32)
            for b in range(NUM_BUCKETS - 2, -1, -1):
                val = jnp.where(bucket == b, (rb_ref[h * NUM_BUCKETS + b] - far) * LOG2E, val)
            if which == 0:
                val = jnp.where(dist >= 0, val, NEG)
            tab[which] = val

    @pl.when(ki == 0)
    def _():
        q = q_ref[...] * ((A_DQK ** -0.5) * LOG2E)
        lane = lax.broadcasted_iota(jnp.int32, (tq, 2 * A_DQK), 1)
        qs[0:tq, :] = jnp.where(lane < A_DQK, q, 0.0).astype(BF16)
        qs[tq:2 * tq, :] = jnp.where(lane >= A_DQK, q, 0.0).astype(BF16)
        m_scr[...] = jnp.full(m_scr.shape, NEG, F32)
        l_scr[...] = jnp.zeros(l_scr.shape, F32)
        acc[...] = jnp.zeros(acc.shape, F32)

    def step(tiles):
        n_chunks = 2 * tq // cw
        k16 = {hf: k_ref[hf * tq:(hf + 1) * tq, :].astype(BF16) for _, hf in tiles}
        vt16 = {hf: vt_ref[:, hf * tq:(hf + 1) * tq] for _, hf in tiles}
        stream = [(which, hf, c) for which, hf in tiles for c in range(n_chunks)]
        s_next = _dot_nt(k16[stream[0][1]], qs[0:cw, :])
        for i, (which, hf, c) in enumerate(stream):
            cols = slice(c * cw, (c + 1) * cw)
            s = s_next
            if i + 1 < len(stream):
                _, hf_n, c_n = stream[i + 1]
                s_next = _dot_nt(k16[hf_n], qs[c_n * cw:(c_n + 1) * cw, :])
            if which is not None:
                q0 = (c * cw) % tq
                s = s + tab[which, :, q0:q0 + cw]
            m_prev = m_scr[0:1, cols]
            m_new = jnp.maximum(m_prev, jnp.max(s, axis=0, keepdims=True))
            alpha = jnp.exp2(m_prev - m_new)
            p = jnp.exp2(s - m_new)
            l_scr[0:1, cols] = alpha * l_scr[0:1, cols] + jnp.sum(p, axis=0, keepdims=True)
            acc[:, cols] = alpha * acc[:, cols] + _dot(vt16[hf], p.astype(BF16))
            m_scr[0:1, cols] = m_new

    all_far = kt * ki + kt - 1 < qi - 1

    @pl.when(all_far)
    def _():
        step([(None, hf) for hf in range(kt)])

    for hf in range(kt):
        kk = kt * ki + hf

        @pl.when((kk < qi - 1) & jnp.logical_not(all_far))
        def _(hf=hf):
            step([(None, hf)])

        @pl.when(kk == qi - 1)
        def _(hf=hf):
            step([(1, hf)])

        @pl.when(kk == qi)
        def _(hf=hf):
            step([(0, hf)])

    @pl.when(ki == qi // kt)
    def _():
        o = acc[...] / l_scr[0:1, :]
        lam = _diff_lambda(lp_ref[...], lam_init)
        a = o[:, 0:tq] - lam * o[:, tq:2 * tq]
        a = a * lax.rsqrt(jnp.mean(a * a, axis=0, keepdims=True) + EPS)
        o_ref[...] = (a.T * g_ref[...] * (1.0 - lam_init)).astype(o_ref.dtype)


def _attn_prompt(u_all, nb, t_len, rel_bias_flat, a_lambda, subln_g, layer, name):
    tq = _tile(t_len, 512, LANE)
    nq = t_len // tq
    lam_init = _lambda_init(layer)
    mp = nb * t_len
    vt = u_all[:mp, C_AV:C_AV + A_W].astype(BF16).reshape(nb, t_len, A_HEADS, A_DV)
    vt = vt.transpose(0, 2, 3, 1).reshape(nb * A_HEADS * A_DV, t_len)
    kt = next(c for c in (4, 2, 1) if nq % c == 0)
    pairs = [(qi, ki) for qi in range(nq) for ki in range(qi // kt + 1)]
    qi_of = jnp.array([p[0] for p in pairs], jnp.int32)
    ki_of = jnp.array([p[1] for p in pairs], jnp.int32)
    grid_spec = pltpu.PrefetchScalarGridSpec(
        num_scalar_prefetch=2,
        grid=(nb, A_HEADS, len(pairs)),
        in_specs=[
            pl.BlockSpec(memory_space=pltpu.SMEM),
            pl.BlockSpec((tq, A_DV), lambda b, h, p, qi, ki: (b * nq + qi[p], C_AQ // A_DV + h)),
            pl.BlockSpec((kt * tq, A_DV), lambda b, h, p, qi, ki: (b * (nq // kt) + ki[p], C_AK // A_DV + h)),
            pl.BlockSpec((A_DV, kt * tq), lambda b, h, p, qi, ki: (b * A_HEADS + h, ki[p])),
            pl.BlockSpec((4, A_DQK), lambda b, h, p, qi, ki: (0, 0)),
            pl.BlockSpec((1, A_DV), lambda b, h, p, qi, ki: (0, 0)),
        ],
        out_specs=pl.BlockSpec((tq, A_DV), lambda b, h, p, qi, ki: (b * nq + qi[p], h)),
        scratch_shapes=[pltpu.VMEM((2 * tq, A_DV), BF16),
                        pltpu.VMEM((SUBLANE, 2 * tq), F32),
                        pltpu.VMEM((SUBLANE, 2 * tq), F32),
                        pltpu.VMEM((A_DV, 2 * tq), F32),
                        pltpu.VMEM((2, tq, tq), F32)],
    )
    return pl.pallas_call(
        functools.partial(_attn_prompt_kernel, tq=tq, kt=kt, lam_init=lam_init),
        out_shape=jax.ShapeDtypeStruct((mp, A_W), BF16),
        grid_spec=grid_spec,
        compiler_params=_params("arbitrary", "arbitrary", "arbitrary"),
        name=name,
    )(qi_of, ki_of, rel_bias_flat, u_all, u_all, vt, a_lambda, subln_g)


def _attn_sample_kernel(pt_ref, rb_ref, q_ref, kn_ref, vn_ref, lp_ref, g_ref, *rest,
                        n_pg, t_len, page, past, lam_init):
    kp = rest[0:n_pg]
    vp = rest[n_pg:2 * n_pg]
    o_ref = rest[2 * n_pg]
    qs, m_scr, l_scr, acc, knp, vnp = rest[2 * n_pg + 1:]
    s_id = pl.program_id(1)
    n_steps = pl.num_programs(1)
    n_cache = n_steps - 1
    rows = A_HEADS * 2 * t_len
    cols = page * A_HEADS
    row = lax.broadcasted_iota(jnp.int32, (rows, 1), 0)
    row_h = row // (2 * t_len)
    row_q = row % t_len

    def head_col(b):
        c = jnp.full((rows, 1), rb_ref[b], F32)
        for hh in range(1, A_HEADS):
            c = jnp.where(row_h == hh, rb_ref[hh * NUM_BUCKETS + b], c)
        return c

    def exact_bias(dist):
        bucket = _t5_bucket(dist)
        far = head_col(NUM_BUCKETS - 1)
        val = jnp.zeros(dist.shape, F32)
        for b in range(NUM_BUCKETS - 2, -1, -1):
            val = jnp.where(bucket == b, (head_col(b) - far) * LOG2E, val)
        return val

    @pl.when(s_id == 0)
    def _():
        q = q_ref[...] * ((A_DQK ** -0.5) * LOG2E)
        lane = lax.broadcasted_iota(jnp.int32, (t_len, A_DV), 1)
        for hh in range(A_HEADS):
            qh = q[:, hh * A_DV:(hh + 1) * A_DV]
            qs[hh * 2 * t_len:hh * 2 * t_len + t_len, :] = jnp.where(lane < A_DQK, qh, 0.0).astype(BF16)
            qs[hh * 2 * t_len + t_len:(hh + 1) * 2 * t_len, :] = jnp.where(lane >= A_DQK, qh, 0.0).astype(BF16)
        m_scr[...] = jnp.full(m_scr.shape, NEG, F32)
        l_scr[...] = jnp.zeros(l_scr.shape, F32)
        acc[...] = jnp.zeros(acc.shape, F32)

    def head_rows(refs, hh):
        return jnp.concatenate([r[pl.ds(hh, page, stride=A_HEADS), :].astype(BF16) for r in refs], axis=0)

    def cache_step(exact):
        n_keys = n_pg * page
        s_all = jnp.concatenate(
            [_dot_nt(qs[hh * 2 * t_len:(hh + 1) * 2 * t_len, :], head_rows(kp, hh)) for hh in range(A_HEADS)],
            axis=0)
        if exact:
            k_pos = (s_id - 1) * n_keys + lax.broadcasted_iota(jnp.int32, (rows, n_keys), 1)
            s_all = s_all + exact_bias(past + row_q - k_pos)

        def pv(p):
            return jnp.concatenate(
                [_dot(p[hh * 2 * t_len:(hh + 1) * 2 * t_len, :].astype(BF16), head_rows(vp, hh))
                 for hh in range(A_HEADS)], axis=0)

        _softmax_step(s_all, m_scr, l_scr, acc, pv)

    @pl.when((s_id >= 1) & (s_id < n_cache))
    def _():
        cache_step(False)

    @pl.when(s_id == n_cache)
    def _():
        cache_step(True)

    @pl.when(s_id == 0)
    def _():
        knp[...] = jnp.zeros(knp.shape, BF16)
        vnp[...] = jnp.zeros(vnp.shape, BF16)
        kn = kn_ref[...]
        vn = vn_ref[...]
        for hh in range(A_HEADS):
            knp[hh, 0:t_len, :] = kn[:, hh * A_DV:(hh + 1) * A_DV].astype(BF16)
            vnp[hh, 0:t_len, :] = vn[:, hh * A_DV:(hh + 1) * A_DV].astype(BF16)
        kcol = lax.broadcasted_iota(jnp.int32, (rows, page), 1)
        dist = row_q - kcol
        bias = jnp.where((dist >= 0) & (kcol < t_len), exact_bias(dist), NEG)
        parts = []
        for hh in range(A_HEADS):
            parts.append(_dot_nt(qs[hh * 2 * t_len:(hh + 1) * 2 * t_len, :], knp[hh]))
        s = jnp.concatenate(parts, axis=0) + bias

        def pv(p):
            outs = []
            for hh in range(A_HEADS):
                outs.append(_dot(p[hh * 2 * t_len:(hh + 1) * 2 * t_len, :].astype(BF16), vnp[hh]))
            return jnp.concatenate(outs, axis=0)

        _softmax_step(s, m_scr, l_scr, acc, pv)

    @pl.when(s_id == n_cache)
    def _():
        o = acc[...] / l_scr[...]
        lam = _diff_lambda(lp_ref[...], lam_init)
        g = g_ref[...]
        for hh in range(A_HEADS):
            r0 = hh * 2 * t_len
            a = o[r0:r0 + t_len, :] - lam * o[r0 + t_len:r0 + 2 * t_len, :]
            o_ref[:, hh * A_DV:(hh + 1) * A_DV] = (_rms(a, g) * (1.0 - lam_init)).astype(o_ref.dtype)


def _attn_sample(u_all, row0, nb, t_len, cache_k, cache_v, page_table, rel_bias_flat, a_lambda,
                 subln_g, layer, name):
    depth, n_pool, page = cache_k.shape[0], cache_k.shape[1], cache_k.shape[2]
    n_pages = page_table.shape[1]
    n_pg = _tile(n_pages, 16, 1)
    n_cache = n_pages // n_pg
    past = n_pages * page
    rb0 = row0 // t_len
    rows = A_HEADS * 2 * t_len
    ck = cache_k.reshape(depth, n_pool, page * A_HEADS, A_DV)
    cv = cache_v.reshape(depth, n_pool, page * A_HEADS, A_DV)
    pt = page_table.reshape(-1)

    def page_spec(i):
        def imap(b, s, pt_ref):
            step = jnp.maximum(s - 1, 0)
            return (layer, pt_ref[b * n_pages + step * n_pg + i], 0, 0)
        return pl.BlockSpec((None, None, page * A_HEADS, A_DV), imap)

    grid_spec = pltpu.PrefetchScalarGridSpec(
        num_scalar_prefetch=1,
        grid=(nb, n_cache + 1),
        in_specs=[
            pl.BlockSpec(memory_space=pltpu.SMEM),
            pl.BlockSpec((t_len, A_W), lambda b, s, pt_ref: (rb0 + b, C_AQ // A_W)),
            pl.BlockSpec((t_len, A_W), lambda b, s, pt_ref: (rb0 + b, C_AK // A_W)),
            pl.BlockSpec((t_len, A_W), lambda b, s, pt_ref: (rb0 + b, C_AV // A_W)),
            pl.BlockSpec((4, A_DQK), lambda b, s, pt_ref: (0, 0)),
            pl.BlockSpec((1, A_DV), lambda b, s, pt_ref: (0, 0)),
        ] + [page_spec(i) for i in range(n_pg)] + [page_spec(i) for i in range(n_pg)],
        out_specs=pl.BlockSpec((t_len, A_W), lambda b, s, pt_ref: (b, 0)),
        scratch_shapes=[pltpu.VMEM((rows, A_DV), BF16),
                        pltpu.VMEM((rows, 1), F32),
                        pltpu.VMEM((rows, 1), F32),
                        pltpu.VMEM((rows, A_DV), F32),
                        pltpu.VMEM((A_HEADS, page, A_DV), BF16),
                        pltpu.VMEM((A_HEADS, page, A_DV), BF16)],
    )
    return pl.pallas_call(
        functools.partial(_attn_sample_kernel, n_pg=n_pg, t_len=t_len, page=page, past=past,
                          lam_init=_lambda_init(layer)),
        out_shape=jax.ShapeDtypeStruct((nb * t_len, A_W), BF16),
        grid_spec=grid_spec,
        compiler_params=_params("arbitrary", "arbitrary"),
        name=name,
    )(pt, rel_bias_flat, u_all, u_all, u_all, a_lambda, subln_g, *([ck] * n_pg), *([cv] * n_pg))


def _merge_kernel(x_ref, og_ref, oa_ref, ol_ref, gt_ref, wg_ref, wa_ref, wl_ref, wo_ref, o_ref):
    d = x_ref.shape[1]
    gt = gt_ref[...].astype(F32)
    m = gt[:, 0:d] * _dot(og_ref[...], wg_ref[...])
    m = m + gt[:, d:2 * d] * _dot(oa_ref[...], wa_ref[...])
    m = m + gt[:, 2 * d:3 * d] * _dot(ol_ref[...], wl_ref[...])
    o_ref[...] = x_ref[...] + _dot(m.astype(BF16), wo_ref[...])


def _merge(x, o_g, o_a, o_l, gates, wg, wa, wl, wo, name):
    m, d = x.shape
    tm = _tile(m, 640)
    row = lambda w: pl.BlockSpec((tm, w), lambda i: (i, 0))
    full = lambda a: pl.BlockSpec(a.shape, lambda i: (0, 0))
    return pl.pallas_call(
        _merge_kernel,
        out_shape=jax.ShapeDtypeStruct((m, d), F32),
        grid=(m // tm,),
        in_specs=[row(d), row(G_W), row(A_W), row(L_WIDTH), row(3 * d),
                  full(wg), full(wa), full(wl), full(wo)],
        out_specs=row(d),
        compiler_params=_params("parallel"),
        name=name,
    )(x, o_g, o_a, o_l, gates, wg, wa, wl, wo)


def _ffn_kernel(x_ref, g_ref, w1_ref, w3_ref, w2_ref, o_ref, h_scr, acc):
    f = pl.program_id(1)

    @pl.when(f == 0)
    def _():
        x = x_ref[...]
        h_scr[...] = _rms(x, g_ref[...]).astype(BF16)
        acc[...] = x

    h = h_scr[...]
    t = _silu(_dot(h, w1_ref[...])) * _dot(h, w3_ref[...])
    acc[...] += _dot(t.astype(BF16), w2_ref[...])

    @pl.when(f == pl.num_programs(1) - 1)
    def _():
        o_ref[...] = acc[...]


def _ffn(x, g, w1, w3, w2, name):
    m, d = x.shape
    ff = w1.shape[1]
    tm = _tile(m, 1280)
    tf = _tile(ff, 512, LANE)
    return pl.pallas_call(
        _ffn_kernel,
        out_shape=jax.ShapeDtypeStruct((m, d), F32),
        grid=(m // tm, ff // tf),
        in_specs=[
            pl.BlockSpec((tm, d), lambda i, f: (i, 0)),
            pl.BlockSpec((1, d), lambda i, f: (0, 0)),
            pl.BlockSpec((d, tf), lambda i, f: (0, f)),
            pl.BlockSpec((d, tf), lambda i, f: (0, f)),
            pl.BlockSpec((tf, d), lambda i, f: (f, 0)),
        ],
        out_specs=pl.BlockSpec((tm, d), lambda i, f: (i, 0)),
        scratch_shapes=[pltpu.VMEM((tm, d), BF16), pltpu.VMEM((tm, d), F32)],
        compiler_params=_params("parallel", "arbitrary"),
        name=name,
    )(x, g, w1, w3, w2)


def _router_kernel(x_ref, g_ref, wr_ref, h_ref, o_ref):
    h = _rms(x_ref[...], g_ref[...])
    h_ref[...] = h.astype(BF16)
    wr_hi, wr_lo = _split(wr_ref[...])
    logits = _dot_split(h, wr_hi, wr_lo)
    lane = lax.broadcasted_iota(jnp.int32, logits.shape, 1)
    logits = jnp.where(lane < N_EXPERTS, logits, -jnp.inf)
    m1 = jnp.max(logits, axis=-1, keepdims=True)
    i1 = jnp.min(jnp.where(logits == m1, lane, LANE), axis=-1, keepdims=True)
    rest = jnp.where(lane == i1, -jnp.inf, logits)
    m2 = jnp.max(rest, axis=-1, keepdims=True)
    i2 = jnp.min(jnp.where(rest == m2, lane, LANE), axis=-1, keepdims=True)
    e2 = jnp.exp(m2 - m1)
    den = 1.0 + e2
    o_ref[...] = jnp.where(lane == 0, i1.astype(F32),
                           jnp.where(lane == 1, i2.astype(F32),
                                     jnp.where(lane == 2, 1.0 / den, jnp.where(lane == 3, e2 / den, 0.0))))


def _router(x, g, wr, name):
    m, d = x.shape
    tm = _tile(m, 1280)
    return pl.pallas_call(
        _router_kernel,
        out_shape=(jax.ShapeDtypeStruct((m, d), BF16), jax.ShapeDtypeStruct((m, LANE), F32)),
        grid=(m // tm,),
        in_specs=[pl.BlockSpec((tm, d), lambda i: (i, 0)),
                  pl.BlockSpec((1, d), lambda i: (0, 0)),
                  pl.BlockSpec((d, LANE), lambda i: (0, 0))],
        out_specs=(pl.BlockSpec((tm, d), lambda i: (i, 0)), pl.BlockSpec((tm, LANE), lambda i: (i, 0))),
        compiler_params=_params("parallel"),
        name=name,
    )(x, g, wr)


MOE_ROW_BLOCK = 640


def _expert_ffn_kernel(be_ref, na_ref, x_ref, w1_ref, w3_ref, w2_ref, o_ref, acc):
    b = pl.program_id(0)
    f = pl.program_id(1)
    active = b < na_ref[0]

    def partial_out():
        x = x_ref[...]
        t = _silu(_dot(x, w1_ref[...])) * _dot(x, w3_ref[...])
        return _dot(t.astype(BF16), w2_ref[...])

    @pl.when(active & (f == 0))
    def _():
        acc[...] = partial_out()

    @pl.when(active & (f > 0))
    def _():
        acc[...] += partial_out()

    @pl.when(jnp.logical_not(active) & (f == 0))
    def _():
        acc[...] = jnp.zeros(acc.shape, F32)

    @pl.when(f == pl.num_programs(1) - 1)
    def _():
        o_ref[...] = acc[...].astype(o_ref.dtype)


def _expert_ffn(xs, block_expert, n_active, w1, w3, w2, name):
    n, d = xs.shape
    ff = w1.shape[2]
    rb = MOE_ROW_BLOCK
    tf = _tile(ff, 1792, LANE)
    grid_spec = pltpu.PrefetchScalarGridSpec(
        num_scalar_prefetch=2,
        grid=(n // rb, ff // tf),
        in_specs=[
            pl.BlockSpec((rb, d), lambda b, f, be, na: (b, 0)),
            pl.BlockSpec((None, d, tf), lambda b, f, be, na: (be[b], 0, f)),
            pl.BlockSpec((None, d, tf), lambda b, f, be, na: (be[b], 0, f)),
            pl.BlockSpec((None, tf, d), lambda b, f, be, na: (be[b], f, 0)),
        ],
        out_specs=pl.BlockSpec((rb, d), lambda b, f, be, na: (b, 0)),
        scratch_shapes=[pltpu.VMEM((rb, d), F32)],
    )
    return pl.pallas_call(
        _expert_ffn_kernel,
        out_shape=jax.ShapeDtypeStruct((n, d), BF16),
        grid_spec=grid_spec,
        compiler_params=_params("arbitrary", "arbitrary"),
        name=name,
    )(block_expert, n_active, xs, w1, w3, w2)


def _combine_kernel(x_ref, y1_ref, y2_ref, r_ref, fg_ref, o_ref):
    r = r_ref[...]
    lane = lax.broadcasted_iota(jnp.int32, r.shape, 1)
    w1 = jnp.sum(jnp.where(lane == 2, r, 0.0), axis=-1, keepdims=True)
    w2 = jnp.sum(jnp.where(lane == 3, r, 0.0), axis=-1, keepdims=True)
    y = w1 * y1_ref[...].astype(F32) + w2 * y2_ref[...].astype(F32)
    o_ref[...] = _rms(x_ref[...] + y, fg_ref[...])


def _combine(x, y1, y2, route, final_g, name):
    m, d = x.shape
    tm = _tile(m, 640)
    row = lambda w: pl.BlockSpec((tm, w), lambda i: (i, 0))
    return pl.pallas_call(
        _combine_kernel,
        out_shape=jax.ShapeDtypeStruct((m, d), F32),
        grid=(m // tm,),
        in_specs=[row(d), row(d), row(d), row(LANE), pl.BlockSpec((1, d), lambda i: (0, 0))],
        out_specs=row(d),
        compiler_params=_params("parallel"),
        name=name,
    )(x, y1, y2, route, final_g)


def _moe_routed(x, g, wr, w1, w3, w2, final_g, layer):
    m, d = x.shape
    ne = w1.shape[0]
    rb = MOE_ROW_BLOCK
    h16, route = _router(x, g, wr, f"router_{layer}")
    eid = route[:, 0:TOP_K].astype(jnp.int32).reshape(-1)
    n_assign = TOP_K * m
    onehot = (eid[:, None] == jnp.arange(ne, dtype=jnp.int32)[None, :]).astype(jnp.int32)
    csum = jnp.cumsum(onehot, axis=0)
    counts = csum[n_assign - 1]
    starts = jnp.cumsum(counts) - counts
    padded = (counts + rb - 1) // rb * rb
    pad_end = jnp.cumsum(padded)
    pad_start = pad_end - padded
    rank = jnp.sum(onehot * csum, axis=1) - 1
    pos = pad_start[eid] + rank
    n_rows = (n_assign + rb - 1) // rb * rb + ne * rb
    block_start = jnp.arange(n_rows // rb, dtype=jnp.int32) * rb
    block_expert = jnp.minimum(jnp.searchsorted(pad_end, block_start, side="right"), ne - 1).astype(jnp.int32)
    n_active = (pad_end[ne - 1:ne] // rb).astype(jnp.int32)
    order = jnp.argsort(eid, stable=True)
    row_e = jnp.repeat(block_expert, rb)
    row_r = jnp.arange(n_rows, dtype=jnp.int32) - pad_start[row_e]
    src = order[jnp.minimum(starts[row_e] + row_r, n_assign - 1)] // TOP_K
    src_tok = jnp.where(row_r < counts[row_e], src, 0)
    xs = jnp.take(h16, src_tok, axis=0)
    ys = _expert_ffn(xs, block_expert, n_active, w1, w3, w2, f"moe_{layer}")
    pos = pos.reshape(m, TOP_K)
    y1 = jnp.take(ys, pos[:, 0], axis=0)
    y2 = jnp.take(ys, pos[:, 1], axis=0)
    return _combine(x, y1, y2, route, final_g, f"moe_combine_{layer}")


def _final_norm_kernel(x_ref, g_ref, o_ref):
    o_ref[...] = _rms(x_ref[...], g_ref[...])


def _final_norm(x, g, name):
    m, d = x.shape
    tm = _tile(m, 1280)
    return pl.pallas_call(
        _final_norm_kernel,
        out_shape=jax.ShapeDtypeStruct((m, d), F32),
        grid=(m // tm,),
        in_specs=[pl.BlockSpec((tm, d), lambda i: (i, 0)), pl.BlockSpec((1, d), lambda i: (0, 0))],
        out_specs=pl.BlockSpec((tm, d), lambda i: (i, 0)),
        compiler_params=_params("parallel"),
        name=name,
    )(x, g)


def _split_w_in(w):
    o_gb = 4 * G_W
    o_aq = o_gb + 2 * G_HEADS
    pad = jnp.zeros((w.shape[0], LANE - 2 * G_HEADS), w.dtype)
    return (jnp.concatenate([w[:, :o_gb], w[:, o_aq:]], axis=1),
            jnp.concatenate([w[:, o_gb:o_aq], pad], axis=1))


def _block_diag(w):
    n, d, _ = w.shape
    eye = jnp.eye(n, dtype=w.dtype)
    return (eye[:, None, :, None] * w[:, :, None, :]).reshape(n * d, n * d)


def kernel(x_prompt, x_sample, cache_k, cache_v, page_table, state_gdn, state_gdn_conv, state_lru, state_lru_conv, norm1_g, norm2_g, final_g, w_in, w_gate, b_gate, g_conv_w, g_a_log, g_dt_bias, g_norm_g, a_lambda, a_subln_g, rel_bias, l_conv_w, l_conv_b, l_wa, l_ba, l_wx, l_bx, l_lambda, w_br_g, w_br_a, w_br_l, w_out, f_w1, f_w3, f_w2, m_router, m_w1, m_w3, m_w2):
    bp, tp, d = x_prompt.shape
    bs, ts, _ = x_sample.shape
    depth = w_in.shape[0]
    mp = bp * tp
    ms = bs * ts
    x = jnp.concatenate([x_prompt.reshape(mp, d), x_sample.reshape(ms, d)], axis=0)
    rb_flat = rel_bias.T.reshape(-1)
    row = lambda v: v.reshape(1, -1)

    k_p, v_p, k_s, v_s = [], [], [], []
    sg_p, sg_s, cg_p, cg_s, sl_p, sl_s, cl_p, cl_s = [], [], [], [], [], [], [], []
    y = None
    for l in range(depth):
        w_main, w_gba = _split_w_in(w_in[l])
        u = _norm_matmul(x, row(norm1_g[l]), w_main.astype(BF16), jnp.zeros((1, U_COLS), F32),
                         None, F32, f"in_proj_{l}")
        gba = _norm_matmul(x, row(norm1_g[l]), w_gba.astype(BF16), jnp.zeros((1, LANE), F32),
                           None, F32, f"in_proj_gba_{l}")
        gates = _norm_matmul(x, row(norm1_g[l]), w_gate[l].astype(BF16), row(b_gate[l]),
                             "sigmoid", BF16, f"gate_proj_{l}")

        ab = jnp.zeros((2, LANE), F32)
        ab = ab.at[0, G_HEADS:2 * G_HEADS].set(g_a_log[l]).at[1, G_HEADS:2 * G_HEADS].set(g_dt_bias[l])
        ng = row(g_norm_g[l])
        og_p, s_p = _gdn(u, gba, 0, bp, tp, jnp.zeros((bp, CONV_W - 1, 3 * G_W), F32),
                         jnp.zeros((bp, G_HEADS, G_DK, G_DV), F32), g_conv_w[l], ab, ng, f"gdn_prompt_{l}")
        og_s, s_s = _gdn(u, gba, mp, bs, ts, state_gdn_conv[l], state_gdn[l], g_conv_w[l], ab, ng,
                         f"gdn_sample_{l}")

        sub_g = row(a_subln_g[l])
        oa_p = _attn_prompt(u, bp, tp, rb_flat, a_lambda[l], sub_g, l, f"attn_prompt_{l}")
        oa_s = _attn_sample(u, mp, bs, ts, cache_k, cache_v, page_table, rb_flat, a_lambda[l], sub_g, l,
                            f"attn_sample_{l}")

        wa_bd = _block_diag(l_wa[l]).astype(BF16)
        wx_bd = _block_diag(l_wx[l]).astype(BF16)
        lru_args = (l_conv_w[l], row(l_conv_b[l]), wa_bd, row(l_ba[l]), wx_bd, row(l_bx[l]), row(l_lambda[l]))
        ol_p, h_p = _lru(u, 0, bp, tp, jnp.zeros((bp, CONV_W - 1, L_WIDTH), F32),
                         jnp.zeros((bp, L_WIDTH), F32), *lru_args, f"lru_prompt_{l}")
        ol_s, h_s = _lru(u, mp, bs, ts, state_lru_conv[l], state_lru[l], *lru_args, f"lru_sample_{l}")

        x = _merge(x, jnp.concatenate([og_p, og_s]), jnp.concatenate([oa_p, oa_s]),
                   jnp.concatenate([ol_p, ol_s]), gates, w_br_g[l].astype(BF16), w_br_a[l].astype(BF16),
                   w_br_l[l].astype(BF16), w_out[l].astype(BF16), f"merge_{l}")

        j = l // 2
        if l % 2 == 0:
            x = _ffn(x, row(norm2_g[l]), f_w1[j].astype(BF16), f_w3[j].astype(BF16), f_w2[j].astype(BF16),
                     f"ffn_{l}")
            if l == depth - 1:
                y = _final_norm(x, row(final_g), "final_norm")
        else:
            wr = jnp.zeros((d, LANE), F32).at[:, :N_EXPERTS].set(m_router[j])
            if l != depth - 1:
                raise NotImplementedError("mixture layer is only fused with the final norm")
            y = _moe_routed(x, row(norm2_g[l]), wr, m_w1[j].astype(BF16), m_w3[j].astype(BF16),
                            m_w2[j].astype(BF16), row(final_g), l)

        def tail_rows(c0, width):
            keep = CONV_W - 1
            p = jnp.stack([u[(b + 1) * tp - keep:(b + 1) * tp, c0:c0 + width] for b in range(bp)])
            s = u[mp:, c0:c0 + width].reshape(bs, ts, width)[:, ts - keep:]
            return p, s

        k_p.append(u[:mp, C_AK:C_AK + A_W].reshape(bp, tp, A_HEADS, A_DV))
        v_p.append(u[:mp, C_AV:C_AV + A_W].reshape(bp, tp, A_HEADS, A_DV))
        k_s.append(u[mp:, C_AK:C_AK + A_W].reshape(bs, ts, A_HEADS, A_DV))
        v_s.append(u[mp:, C_AV:C_AV + A_W].reshape(bs, ts, A_HEADS, A_DV))
        sg_p.append(s_p)
        sg_s.append(s_s)
        cp, cs = tail_rows(C_GQKV, 3 * G_W)
        cg_p.append(cp)
        cg_s.append(cs)
        sl_p.append(h_p)
        sl_s.append(h_s)
        cp, cs = tail_rows(C_LX, L_WIDTH)
        cl_p.append(cp)
        cl_s.append(cs)

    st = jnp.stack
    return (y[:mp].reshape(bp, tp, d), y[mp:].reshape(bs, ts, d),
            st(k_p), st(v_p), st(k_s), st(v_s), st(sg_p), st(sg_s), st(cg_p), st(cg_s),
            st(sl_p), st(sl_s), st(cl_p), st(cl_s))
```
